```python
import math
import jax, jax.numpy as jnp
from jax import lax
import numpy as np

D_MODEL = 2048
BATCH = 16
SEQ = 2048
DEPTH = 4
DEC_BATCH = 2
DEC_SEQ = 16384
PAST_LEN = 128

HEAD_DIM = 128
CHUNK = 128
Q_BLOCK = 128
A_GROUPS = D_MODEL // (2 * HEAD_DIM)
A_WIDTH = A_GROUPS * HEAD_DIM
B_HEADS = D_MODEL // (2 * HEAD_DIM)
B_QK_DIM = HEAD_DIM // 2
B_V_DIM = HEAD_DIM
B_QK_WIDTH = 2 * B_HEADS * B_QK_DIM
B_WIDTH = B_HEADS * B_V_DIM
AB_IN = 2 * A_WIDTH + 2 * B_QK_WIDTH + B_WIDTH
AB_OUT = A_WIDTH + B_WIDTH
C_HEADS = D_MODEL // HEAD_DIM
C_KV_HEADS = 4
C_GROUP = C_HEADS // C_KV_HEADS
C_WINDOW = 128
C_IN = (C_HEADS + 2 * C_KV_HEADS) * HEAD_DIM
C_OUT = C_HEADS * HEAD_DIM
D_FF = 4 * D_MODEL
ROPE_THETA = 500000.0
ROPE_FRACTION = 4
N_AB_LAYERS = (DEPTH + 1) // 2
N_C_LAYERS = DEPTH // 2
EPS = 1e-6

kernel_name = 'hybrid_gmlp_diffattn_swa_encoder'


def rms_norm(x, g):
    xf = x.astype(jnp.float32)
    y = xf * lax.rsqrt(jnp.mean(xf * xf, axis=-1, keepdims=True) + EPS)
    return (y * g.astype(jnp.float32)).astype(x.dtype)


def partial_rope(x, rot_dim):
    S = x.shape[1]
    half = rot_dim // 2
    inv = ROPE_THETA ** (-jnp.arange(half, dtype=jnp.float32) / half)
    ang = jnp.arange(S, dtype=jnp.float32)[:, None] * inv[None, :]
    cos = jnp.cos(ang)[None, :, None, :].astype(x.dtype)
    sin = jnp.sin(ang)[None, :, None, :].astype(x.dtype)
    x1, x2, rest = x[..., :half], x[..., half:rot_dim], x[..., rot_dim:]
    return jnp.concatenate([x1 * cos - x2 * sin, x2 * cos + x1 * sin, rest], axis=-1)


def chunked_gmlp(u, v, vnorm_g, ws, bs):
    B, S, G, C = v.shape
    u = jax.nn.gelu(u)
    v = rms_norm(jax.nn.gelu(v), vnorm_g)
    vc = v.reshape(B, S // CHUNK, CHUNK, G, C)
    mixed = jnp.einsum('gpq,bnqgc->bnpgc', ws, vc) + bs.T[:, :, None]
    return (u * mixed.reshape(B, S, G, C)).reshape(B, S, G * C)


def diff_attention(q, k, v, lam, lam_init, subnorm_g):
    B, S, _, H, dk = q.shape
    dv = v.shape[-1]
    nq = S // Q_BLOCK
    qb = q.reshape(B, nq, Q_BLOCK, 2, H, dk).transpose(1, 0, 2, 3, 4, 5)
    scale = dk ** -0.5

    def block(qblk):
        s = jnp.einsum('bqihd,bkihd->bihqk', qblk, k).astype(jnp.float32) * scale
        p = jax.nn.softmax(s, axis=-1)
        a = p[:, 0] - lam * p[:, 1]
        return jnp.einsum('bhqk,bkhd->bqhd', a.astype(v.dtype), v)

    o = lax.map(block, qb)
    o = o.transpose(1, 0, 2, 3, 4).reshape(B, S, H, dv)
    o = rms_norm(o, subnorm_g) * (1.0 - lam_init)
    return o.reshape(B, S, H * dv)


def window_gqa_sink(q, k, v, sink):
    B, S, HQ, d = q.shape
    nb = S // CHUNK
    qb = q.reshape(B, nb, CHUNK, C_KV_HEADS, C_GROUP, d)

    def band(t):
        tp = jnp.pad(t, ((0, 0), (CHUNK, CHUNK), (0, 0), (0, 0)))
        tp = tp.reshape(B, nb + 2, CHUNK, C_KV_HEADS, d)
        return jnp.concatenate([tp[:, :-2], tp[:, 1:-1], tp[:, 2:]], axis=2)

    kb, vb = band(k), band(v)
    s = jnp.einsum('bnqkgd,bnjkd->bnkgqj', qb, kb).astype(jnp.float32) * (d ** -0.5)
    qpos = jnp.arange(nb)[:, None, None] * CHUNK + jnp.arange(CHUNK)[None, :, None]
    kpos = (jnp.arange(nb)[:, None, None] - 1) * CHUNK + jnp.arange(3 * CHUNK)[None, None, :]
    valid = (jnp.abs(kpos - qpos) <= C_WINDOW) & (kpos >= 0) & (kpos < S)
    s = jnp.where(valid[None, :, None, None], s, -jnp.inf)
    sink_l = sink.astype(jnp.float32).reshape(C_KV_HEADS, C_GROUP)[None, None, :, :, None, None]
    m = jnp.maximum(jnp.max(s, axis=-1, keepdims=True), sink_l)
    p = jnp.exp(s - m)
    p = p / (jnp.sum(p, axis=-1, keepdims=True) + jnp.exp(sink_l - m))
    o = jnp.einsum('bnkgqj,bnjkd->bnqkgd', p.astype(v.dtype), vb)
    return o.reshape(B, S, HQ * d)


def ab_mixer(h, w_in, w_out, vnorm_g, ws, bs, lam_vecs, subnorm_g, lam_init):
    B, S, _ = h.shape
    z = h @ w_in
    u, va, qd, kd, vd = jnp.split(
        z, [A_WIDTH, 2 * A_WIDTH, 2 * A_WIDTH + B_QK_WIDTH, 2 * A_WIDTH + 2 * B_QK_WIDTH], axis=-1)
    a_out = chunked_gmlp(u.reshape(B, S, A_GROUPS, HEAD_DIM), va.reshape(B, S, A_GROUPS, HEAD_DIM),
                         vnorm_g, ws, bs)
    rot = B_QK_DIM // ROPE_FRACTION
    q = partial_rope(qd.reshape(B, S, 2 * B_HEADS, B_QK_DIM), rot).reshape(B, S, 2, B_HEADS, B_QK_DIM)
    k = partial_rope(kd.reshape(B, S, 2 * B_HEADS, B_QK_DIM), rot).reshape(B, S, 2, B_HEADS, B_QK_DIM)
    lv = lam_vecs.astype(jnp.float32)
    lam = jnp.exp(jnp.sum(lv[0] * lv[1])) - jnp.exp(jnp.sum(lv[2] * lv[3])) + lam_init
    b_out = diff_attention(q, k, vd.reshape(B, S, B_HEADS, B_V_DIM), lam, lam_init, subnorm_g)
    return jnp.concatenate([a_out, b_out], axis=-1) @ w_out


def c_mixer(h, w_in, w_out, sink):
    B, S, _ = h.shape
    z = h @ w_in
    qd, kd, vd = jnp.split(z, [C_HEADS * HEAD_DIM, (C_HEADS + C_KV_HEADS) * HEAD_DIM], axis=-1)
    rot = HEAD_DIM // ROPE_FRACTION
    q = partial_rope(qd.reshape(B, S, C_HEADS, HEAD_DIM), rot)
    k = partial_rope(kd.reshape(B, S, C_KV_HEADS, HEAD_DIM), rot)
    v = vd.reshape(B, S, C_KV_HEADS, HEAD_DIM)
    return window_gqa_sink(q, k, v, sink) @ w_out


def trunk(x, norm_mix_pre, norm_mix_post, norm_mlp_pre, norm_mlp_post,
          w_in_ab, w_out_ab, a_vnorm, a_ws, a_bs, b_lambda, b_subnorm,
          w_in_c, w_out_c, c_sink, w_up, w_down):
    for i in range(DEPTH):
        h = rms_norm(x, norm_mix_pre[i])
        j = i // 2
        if i % 2 == 0:
            lam_init = 0.8 - 0.6 * math.exp(-0.3 * i)
            h = ab_mixer(h, w_in_ab[j], w_out_ab[j], a_vnorm[j], a_ws[j], a_bs[j],
                         b_lambda[j], b_subnorm[j], lam_init)
        else:
            h = c_mixer(h, w_in_c[j], w_out_c[j], c_sink[j])
        x = x + rms_norm(h, norm_mix_post[i])
        h = rms_norm(x, norm_mlp_pre[i])
        h = jnp.square(jax.nn.relu(h @ w_up[i])) @ w_down[i]
        x = x + rms_norm(h, norm_mlp_post[i])
    return x


def setup_inputs(seed: int = 0) -> dict:
    key = jax.random.key(seed)
    ks = jax.random.split(key, 20)
    f32 = jnp.float32
    nrm = lambda k, shape, scale: jax.random.normal(k, shape, f32) * scale
    return {
        'x_prompt': nrm(ks[0], (BATCH, SEQ, D_MODEL), 1.0),
        'x_sample': nrm(ks[1], (DEC_BATCH, DEC_SEQ, D_MODEL), 1.0),
        'norm_mix_pre': 1.0 + nrm(ks[2], (DEPTH, D_MODEL), 0.05),
        'norm_mix_post': 1.0 + nrm(ks[3], (DEPTH, D_MODEL), 0.05),
        'norm_mlp_pre': 1.0 + nrm(ks[4], (DEPTH, D_MODEL), 0.05),
        'norm_mlp_post': 1.0 + nrm(ks[5], (DEPTH, D_MODEL), 0.05),
        'w_in_ab': nrm(ks[6], (N_AB_LAYERS, D_MODEL, AB_IN), D_MODEL ** -0.5),
        'w_out_ab': nrm(ks[7], (N_AB_LAYERS, AB_OUT, D_MODEL), AB_OUT ** -0.5),
        'a_vnorm': 1.0 + nrm(ks[8], (N_AB_LAYERS, A_GROUPS, HEAD_DIM), 0.05),
        'a_ws': nrm(ks[9], (N_AB_LAYERS, A_GROUPS, CHUNK, CHUNK), CHUNK ** -0.5),
        'a_bs': 1.0 + nrm(ks[10], (N_AB_LAYERS, A_GROUPS, CHUNK), 0.1),
        'b_lambda': nrm(ks[11], (N_AB_LAYERS, 4, B_QK_DIM), 0.1),
        'b_subnorm': 1.0 + nrm(ks[12], (N_AB_LAYERS, B_V_DIM), 0.05),
        'w_in_c': nrm(ks[13], (N_C_LAYERS, D_MODEL, C_IN), D_MODEL ** -0.5),
        'w_out_c': nrm(ks[14], (N_C_LAYERS, C_OUT, D_MODEL), C_OUT ** -0.5),
        'c_sink': nrm(ks[15], (N_C_LAYERS, C_HEADS), 0.5),
        'w_up': nrm(ks[16], (DEPTH, D_MODEL, D_FF), D_MODEL ** -0.5),
        'w_down': nrm(ks[17], (DEPTH, D_FF, D_MODEL), D_FF ** -0.5),
    }


def reference(x_prompt, x_sample, norm_mix_pre, norm_mix_post, norm_mlp_pre, norm_mlp_post,
              w_in_ab, w_out_ab, a_vnorm, a_ws, a_bs, b_lambda, b_subnorm,
              w_in_c, w_out_c, c_sink, w_up, w_down):
    y_prompt = trunk(x_prompt, norm_mix_pre, norm_mix_post, norm_mlp_pre, norm_mlp_post,
                     w_in_ab, w_out_ab, a_vnorm, a_ws, a_bs, b_lambda, b_subnorm,
                     w_in_c, w_out_c, c_sink, w_up, w_down)
    y_sample = trunk(x_sample, norm_mix_pre, norm_mix_post, norm_mlp_pre, norm_mlp_post,
                     w_in_ab, w_out_ab, a_vnorm, a_ws, a_bs, b_lambda, b_subnorm,
                     w_in_c, w_out_c, c_sink, w_up, w_down)
    return (y_prompt, y_sample)
```

```python
import functools
import math

import jax
import jax.numpy as jnp
from jax import lax
from jax.experimental import pallas as pl
from jax.experimental.pallas import tpu as pltpu

F32 = jnp.float32
BF16 = jnp.bfloat16

D_MODEL = 2048
DEPTH = 4
HEAD_DIM = 128
CHUNK = 128
A_GROUPS = 8
A_WIDTH = 1024
B_HEADS = 8
B_QK_DIM = 64
B_QK_WIDTH = 1024
B_WIDTH = 1024
AB_IN = 5120
C_HEADS = 16
C_KV_HEADS = 4
C_GROUP = 4
C_WINDOW = 128
C_IN = 3072
D_FF = 8192
ROPE_THETA = 500000.0
ROPE_FRACTION = 4
EPS = 1e-6
LOG2E = 1.4426950408889634
NEG_BIG = -1e30

LANES = 128
VMEM_LIMIT = 56 * 1024 * 1024


def _params(sem, limit=VMEM_LIMIT):
    return pltpu.CompilerParams(dimension_semantics=sem, vmem_limit_bytes=limit)


def _rms(x, g):
    return x * lax.rsqrt(jnp.mean(x * x, axis=-1, keepdims=True) + EPS) * g


def _in_proj_kernel(x_ref, g_ref, w_ref, cos_ref, sa_ref, sb_ref, o_ref, h_ref, *,
                    rope_lo, rope_hi, half, bn):
    j = pl.program_id(1)

    @pl.when(j == 0)
    def _():
        h_ref[...] = _rms(x_ref[...], g_ref[...]).astype(BF16)

    z = jnp.dot(h_ref[...], w_ref[...], preferred_element_type=F32)
    is_rope = jnp.logical_and(j >= rope_lo, j < rope_hi)

    @pl.when(is_rope)
    def _():
        cos = cos_ref[...]
        sa = sa_ref[...]
        sb = sb_ref[...]
        for c in range(bn // LANES):
            zc = z[:, c * LANES:(c + 1) * LANES]
            zr = (zc * cos + pltpu.roll(zc, half, 1) * sa
                  + pltpu.roll(zc, LANES - half, 1) * sb)
            o_ref[:, c * LANES:(c + 1) * LANES] = zr.astype(BF16)

    @pl.when(jnp.logical_not(is_rope))
    def _():
        o_ref[...] = z.astype(BF16)


def _in_proj(x, g, w, tables, seq, *, bn, rope_cols, half, bm=512):
    m, d = x.shape
    n = w.shape[1]
    bm = min(bm, seq)
    nsb = seq // bm
    cos, sa, sb = tables
    kern = functools.partial(_in_proj_kernel, rope_lo=rope_cols[0] // bn,
                             rope_hi=rope_cols[1] // bn, half=half, bn=bn)
    tab_spec = pl.BlockSpec((bm, LANES), lambda i, j: (i % nsb, 0))
    return pl.pallas_call(
        kern,
        grid=(m // bm, n // bn),
        in_specs=[
            pl.BlockSpec((bm, d), lambda i, j: (i, 0)),
            pl.BlockSpec((1, d), lambda i, j: (0, 0)),
            pl.BlockSpec((d, bn), lambda i, j: (0, j)),
            tab_spec, tab_spec, tab_spec,
        ],
        out_specs=pl.BlockSpec((bm, bn), lambda i, j: (i, j)),
        out_shape=jax.ShapeDtypeStruct((m, n), BF16),
        scratch_shapes=[pltpu.VMEM((bm, d), BF16)],
        compiler_params=_params(("parallel", "arbitrary")),
        name="in_proj",
    )(x, g, w, cos, sa, sb)


def _rope_tables(seq, head_dim, rot):
    half = rot // 2
    inv = ROPE_THETA ** (-jnp.arange(half, dtype=F32) / half)
    ang = jnp.arange(seq, dtype=F32)[:, None] * inv[None, :]
    c, s = jnp.cos(ang), jnp.sin(ang)
    z_half = jnp.zeros((seq, half), F32)
    z_rest = jnp.zeros((seq, head_dim - rot), F32)
    cos = jnp.concatenate([c, c, jnp.ones((seq, head_dim - rot), F32)], axis=1)
    sa = jnp.concatenate([z_half, s, z_rest], axis=1)
    sb = jnp.concatenate([-s, z_half, z_rest], axis=1)
    reps = LANES // head_dim
    return tuple(jnp.tile(t, (1, reps)) for t in (cos, sa, sb))


def _gmlp_kernel(u_ref, v_ref, vn_ref, ws_ref, bs_ref, o_ref, *, tm):
    for g in range(A_GROUPS):
        cols = slice(g * HEAD_DIM, (g + 1) * HEAD_DIM)
        u = jax.nn.gelu(u_ref[:, cols].astype(F32))
        v = jax.nn.gelu(v_ref[:, cols].astype(F32))
        v = _rms(v, vn_ref[g:g + 1, :]).astype(BF16)
        w = ws_ref[g]
        b = bs_ref[g]
        for c in range(tm // CHUNK):
            rows = slice(c * CHUNK, (c + 1) * CHUNK)
            mixed = jnp.dot(w, v[rows, :], preferred_element_type=F32) + b
            o_ref[rows, cols] = (u[rows, :] * mixed).astype(BF16)


def _gmlp(z, vnorm, ws, bs_b, *, tm=512):
    m = z.shape[0]
    tm = min(tm, m)
    return pl.pallas_call(
        functools.partial(_gmlp_kernel, tm=tm),
        grid=(m // tm,),
        in_specs=[
            pl.BlockSpec((tm, A_WIDTH), lambda i: (i, 0)),
            pl.BlockSpec((tm, A_WIDTH), lambda i: (i, 1)),
            pl.BlockSpec((A_GROUPS, HEAD_DIM), lambda i: (0, 0)),
            pl.BlockSpec((A_GROUPS, CHUNK, CHUNK), lambda i: (0, 0, 0)),
            pl.BlockSpec((A_GROUPS, CHUNK, HEAD_DIM), lambda i: (0, 0, 0)),
        ],
        out_specs=pl.BlockSpec((tm, A_WIDTH), lambda i: (i, 0)),
        out_shape=jax.ShapeDtypeStruct((m, A_WIDTH), BF16),
        compiler_params=_params(("parallel",)),
        name="gmlp",
    )(z, z, vnorm, ws, bs_b)


def _vt_kernel(v_ref, o_ref):
    for h in range(B_HEADS):
        v = v_ref[:, h * HEAD_DIM:(h + 1) * HEAD_DIM].astype(F32)
        o_ref[h] = v.T.astype(BF16)


def _v_transposed(z3, *, tk):
    b, s, _ = z3.shape
    v_block = (2 * A_WIDTH + 2 * B_QK_WIDTH) // B_WIDTH
    return pl.pallas_call(
        _vt_kernel,
        grid=(b, s // tk),
        in_specs=[pl.BlockSpec((None, tk, B_WIDTH), lambda i, c: (i, c, v_block))],
        out_specs=pl.BlockSpec((None, B_HEADS, None, HEAD_DIM, tk), lambda i, c: (i, 0, c, 0, 0)),
        out_shape=jax.ShapeDtypeStruct((b, B_HEADS, s // tk, HEAD_DIM, tk), BF16),
        compiler_params=_params(("parallel", "parallel")),
        name="v_transpose",
    )(z3)


def _diff_attn_kernel(q_ref, k_ref, vt_ref, lam_ref, g_ref, o_ref,
                      qt_ref, m_ref, l_ref, acc_ref, *, tq, tk, nck, lam_init):
    q = q_ref[...].astype(F32) * (B_QK_DIM ** -0.5 * LOG2E)
    qt = q.T
    row = lax.broadcasted_iota(jnp.int32, qt.shape, 0)
    qt_ref[:, :tq] = jnp.where(row < B_QK_DIM, qt, 0.0).astype(BF16)
    qt_ref[:, tq:] = jnp.where(row >= B_QK_DIM, qt, 0.0).astype(BF16)
    m_ref[...] = jnp.full(m_ref.shape, NEG_BIG, F32)
    l_ref[...] = jnp.zeros(l_ref.shape, F32)
    acc_ref[...] = jnp.zeros(acc_ref.shape, F32)

    def body(c, carry):
        kb = k_ref[pl.ds(pl.multiple_of(c * tk, tk), tk), :]
        s = jnp.dot(kb, qt_ref[...], preferred_element_type=F32)
        m_old = m_ref[...]
        m_new = jnp.maximum(m_old, jnp.max(s, axis=0, keepdims=True))
        alpha = jnp.exp2(m_old - m_new)
        p = jnp.exp2(s - m_new)
        l_ref[...] = alpha * l_ref[...] + jnp.sum(p, axis=0, keepdims=True)
        acc_ref[...] = alpha * acc_ref[...] + jnp.dot(
            vt_ref[c], p.astype(BF16), preferred_element_type=F32)
        m_ref[...] = m_new
        return carry

    lax.fori_loop(0, nck, body, 0)

    lv = lam_ref[...]
    lam = (jnp.exp(jnp.sum(lv[0:1] * lv[1:2], axis=1, keepdims=True))
           - jnp.exp(jnp.sum(lv[2:3] * lv[3:4], axis=1, keepdims=True)) + lam_init)
    inv_l = 1.0 / l_ref[...]
    acc = acc_ref[...] * inv_l
    ot = acc[:, :tq] - lam * acc[:, tq:]
    o = ot.T
    o_ref[...] = (_rms(o, g_ref[...]) * (1.0 - lam_init)).astype(BF16)


def _diff_attn(z3, vt, lam_vecs, subnorm, lam_init, *, tq, tk):
    b, s, _ = z3.shape
    nck = s // tk
    q_block = 2 * A_WIDTH // LANES
    k_block = (2 * A_WIDTH + B_QK_WIDTH) // LANES
    kern = functools.partial(_diff_attn_kernel, tq=tq, tk=tk, nck=nck, lam_init=lam_init)
    return pl.pallas_call(
        kern,
        grid=(b, B_HEADS, s // tq),
        in_specs=[
            pl.BlockSpec((None, tq, LANES), lambda i, h, t: (i, t, q_block + h)),
            pl.BlockSpec((None, s, LANES), lambda i, h, t: (i, 0, k_block + h)),
            pl.BlockSpec((None, None, nck, HEAD_DIM, tk), lambda i, h, t: (i, h, 0, 0, 0)),
            pl.BlockSpec((4, B_QK_DIM), lambda i, h, t: (0, 0)),
            pl.BlockSpec((1, HEAD_DIM), lambda i, h, t: (0, 0)),
        ],
        out_specs=pl.BlockSpec((None, tq, HEAD_DIM), lambda i, h, t: (i, t, h)),
        out_shape=jax.ShapeDtypeStruct((b, s, B_WIDTH), BF16),
        scratch_shapes=[
            pltpu.VMEM((HEAD_DIM, 2 * tq), BF16),
            pltpu.VMEM((1, 2 * tq), F32),
            pltpu.VMEM((1, 2 * tq), F32),
            pltpu.VMEM((HEAD_DIM, 2 * tq), F32),
        ],
        compiler_params=_params(("parallel", "parallel", "arbitrary")),
        name="diff_attn",
    )(z3, z3, vt, lam_vecs, subnorm)


def _win_attn_kernel(sink_ref, q_ref, kp_ref, kc_ref, kn_ref, vp_ref, vc_ref, vn_ref, o_ref, *, nb):
    n = pl.program_id(1)
    r = lax.broadcasted_iota(jnp.int32, (CHUNK, 3 * CHUNK), 0)
    c = lax.broadcasted_iota(jnp.int32, (CHUNK, 3 * CHUNK), 1)
    rel = c - CHUNK - r
    valid = jnp.logical_and(rel <= C_WINDOW, rel >= -C_WINDOW)
    valid = jnp.logical_and(valid, jnp.logical_or(n > 0, c >= CHUNK))
    valid = jnp.logical_and(valid, jnp.logical_or(n < nb - 1, c < 2 * CHUNK))
    scale = HEAD_DIM ** -0.5
    for kh in range(C_KV_HEADS):
        kcols = slice(kh * HEAD_DIM, (kh + 1) * HEAD_DIM)
        kband = jnp.concatenate([kp_ref[:, kcols], kc_ref[:, kcols], kn_ref[:, kcols]], axis=0)
        vband = jnp.concatenate([vp_ref[:, kcols], vc_ref[:, kcols], vn_ref[:, kcols]], axis=0)
        for g in range(C_GROUP):
            head = kh * C_GROUP + g
            hcols = slice(head * HEAD_DIM, (head + 1) * HEAD_DIM)
            s = lax.dot_general(q_ref[:, hcols], kband, (((1,), (1,)), ((), ())),
                                preferred_element_type=F32) * scale
            s = jnp.where(valid, s, NEG_BIG)
            sink = sink_ref[head]
            m = jnp.maximum(jnp.max(s, axis=-1, keepdims=True), sink)
            p = jnp.exp(s - m)
            denom = jnp.sum(p, axis=-1, keepdims=True) + jnp.exp(sink - m)
            p = p / denom
            o = jnp.dot(p.astype(BF16), vband, preferred_element_type=F32)
            o_ref[:, hcols] = o.astype(BF16)


def _win_attn(z3, sink):
    b, s, _ = z3.shape
    nb = s // CHUNK
    kv_w = C_KV_HEADS * HEAD_DIM
    k_block = C_HEADS * HEAD_DIM // kv_w
    v_block = k_block + 1
    prev = lambda n: jnp.maximum(n - 1, 0)
    nxt = lambda n: jnp.minimum(n + 1, nb - 1)

    def kv_spec(blk, f):
        return pl.BlockSpec((None, CHUNK, kv_w), lambda i, n, sk: (i, f(n), blk))

    same = lambda n: n
    grid_spec = pltpu.PrefetchScalarGridSpec(
        num_scalar_prefetch=1,
        grid=(b, nb),
        in_specs=[
            pl.BlockSpec((None, CHUNK, C_HEADS * HEAD_DIM), lambda i, n, sk: (i, n, 0)),
            kv_spec(k_block, prev), kv_spec(k_block, same), kv_spec(k_block, nxt),
            kv_spec(v_block, prev), kv_spec(v_block, same), kv_spec(v_block, nxt),
        ],
        out_specs=pl.BlockSpec((None, CHUNK, C_HEADS * HEAD_DIM), lambda i, n, sk: (i, n, 0)),
    )
    return pl.pallas_call(
        functools.partial(_win_attn_kernel, nb=nb),
        grid_spec=grid_spec,
        out_shape=jax.ShapeDtypeStruct((b, s, C_HEADS * HEAD_DIM), BF16),
        compiler_params=_params(("parallel", "parallel")),
        name="win_attn",
    )(sink, z3, z3, z3, z3, z3, z3, z3)


def _out_proj_kernel(*refs, n_in):
    a_refs = refs[:n_in]
    w_refs = refs[n_in:2 * n_in]
    x_ref, g_ref, o_ref = refs[2 * n_in:]
    acc = jnp.dot(a_refs[0][...], w_refs[0][...], preferred_element_type=F32)
    for a_ref, w_ref in zip(a_refs[1:], w_refs[1:]):
        acc = acc + jnp.dot(a_ref[...], w_ref[...], preferred_element_type=F32)
    o_ref[...] = x_ref[...] + _rms(acc, g_ref[...])


def _out_proj(acts, ws, x, g, *, bm=512):
    m, d = x.shape
    bm = min(bm, m)
    n_in = len(acts)
    in_specs = [pl.BlockSpec((bm, a.shape[1]), lambda i: (i, 0)) for a in acts]
    in_specs += [pl.BlockSpec(w.shape, lambda i: (0, 0)) for w in ws]
    in_specs += [pl.BlockSpec((bm, d), lambda i: (i, 0)), pl.BlockSpec((1, d), lambda i: (0, 0))]
    return pl.pallas_call(
        functools.partial(_out_proj_kernel, n_in=n_in),
        grid=(m // bm,),
        in_specs=in_specs,
        out_specs=pl.BlockSpec((bm, d), lambda i: (i, 0)),
        out_shape=jax.ShapeDtypeStruct((m, d), F32),
        compiler_params=_params(("parallel",)),
        name="out_proj",
    )(*acts, *ws, x, g)


def _mlp_kernel(x_ref, g1_ref, wu_ref, wd_ref, g2_ref, o_ref, h_ref, acc_ref):
    f = pl.program_id(1)

    @pl.when(f == 0)
    def _():
        h_ref[...] = _rms(x_ref[...], g1_ref[...]).astype(BF16)

    a = jnp.dot(h_ref[...], wu_ref[...], preferred_element_type=F32)
    a = jnp.square(jnp.maximum(a, 0.0)).astype(BF16)
    part = jnp.dot(a, wd_ref[...], preferred_element_type=F32)

    @pl.when(f == 0)
    def _():
        acc_ref[...] = part

    @pl.when(f > 0)
    def _():
        acc_ref[...] += part

    @pl.when(f == pl.num_programs(1) - 1)
    def _():
        o_ref[...] = x_ref[...] + _rms(acc_ref[...], g2_ref[...])


def _mlp(x, g1, wu, wd, g2, *, bm=512, tf=512):
    m, d = x.shape
    dff = wu.shape[1]
    bm = min(bm, m)
    return pl.pallas_call(
        _mlp_kernel,
        grid=(m // bm, dff // tf),
        in_specs=[
            pl.BlockSpec((bm, d), lambda i, f: (i, 0)),
            pl.BlockSpec((1, d), lambda i, f: (0, 0)),
            pl.BlockSpec((d, tf), lambda i, f: (0, f)),
            pl.BlockSpec((tf, d), lambda i, f: (f, 0)),
            pl.BlockSpec((1, d), lambda i, f: (0, 0)),
        ],
        out_specs=pl.BlockSpec((bm, d), lambda i, f: (i, 0)),
        out_shape=jax.ShapeDtypeStruct((m, d), F32),
        scratch_shapes=[pltpu.VMEM((bm, d), BF16), pltpu.VMEM((bm, d), F32)],
        compiler_params=_params(("parallel", "arbitrary")),
        name="mlp",
    )(x, g1, wu, wd, g2)


def _pair_heads(w):
    d = w.shape[0]
    return w.reshape(d, 2, B_HEADS, B_QK_DIM).transpose(0, 2, 1, 3).reshape(d, B_QK_WIDTH)


def _prep_weights(p):
    row = lambda a: a.reshape(a.shape[0], 1, a.shape[-1]).astype(F32)
    w_in_ab = p["w_in_ab"]
    q0, k0, v0 = 2 * A_WIDTH, 2 * A_WIDTH + B_QK_WIDTH, 2 * A_WIDTH + 2 * B_QK_WIDTH
    w_in_ab = jnp.concatenate([
        w_in_ab[:, :, :q0],
        jax.vmap(_pair_heads)(w_in_ab[:, :, q0:k0]),
        jax.vmap(_pair_heads)(w_in_ab[:, :, k0:v0]),
        w_in_ab[:, :, v0:],
    ], axis=-1).astype(BF16)
    return dict(
        norm_mix_pre=row(p["norm_mix_pre"]), norm_mix_post=row(p["norm_mix_post"]),
        norm_mlp_pre=row(p["norm_mlp_pre"]), norm_mlp_post=row(p["norm_mlp_post"]),
        w_in_ab=w_in_ab,
        w_out_ab=p["w_out_ab"].astype(BF16),
        a_vnorm=p["a_vnorm"].astype(F32),
        a_ws=p["a_ws"].astype(BF16),
        a_bs_b=jnp.broadcast_to(p["a_bs"].astype(F32)[..., None],
                                p["a_bs"].shape + (HEAD_DIM,)),
        b_lambda=p["b_lambda"].astype(F32),
        b_subnorm=p["b_subnorm"].astype(F32)[:, None, :],
        w_in_c=p["w_in_c"].astype(BF16),
        w_out_c=p["w_out_c"].astype(BF16),
        c_sink=p["c_sink"].astype(F32),
        w_up=p["w_up"].astype(BF16),
        w_down=p["w_down"].astype(BF16),
    )


def _trunk(x3, p):
    b, s, d = x3.shape
    m = b * s
    x = x3.reshape(m, d)
    tq = min(256, s)
    tk = min(512, s)
    tab_ab = _rope_tables(s, B_QK_DIM, B_QK_DIM // ROPE_FRACTION)
    tab_c = _rope_tables(s, HEAD_DIM, HEAD_DIM // ROPE_FRACTION)
    for i in range(DEPTH):
        j = i // 2
        if i % 2 == 0:
            lam_init = 0.8 - 0.6 * math.exp(-0.3 * i)
            z = _in_proj(x, p["norm_mix_pre"][i], p["w_in_ab"][j], tab_ab, s, bn=1024,
                         rope_cols=(2 * A_WIDTH, 2 * A_WIDTH + 2 * B_QK_WIDTH),
                         half=B_QK_DIM // ROPE_FRACTION // 2)
            a_out = _gmlp(z, p["a_vnorm"][j], p["a_ws"][j], p["a_bs_b"][j])
            z3 = z.reshape(b, s, AB_IN)
            vt = _v_transposed(z3, tk=tk)
            b_out = _diff_attn(z3, vt, p["b_lambda"][j], p["b_subnorm"][j], lam_init, tq=tq, tk=tk)
            w_out = p["w_out_ab"][j]
            x = _out_proj([a_out, b_out.reshape(m, B_WIDTH)], [w_out[:A_WIDTH], w_out[A_WIDTH:]],
                          x, p["norm_mix_post"][i])
        else:
            z = _in_proj(x, p["norm_mix_pre"][i], p["w_in_c"][j], tab_c, s, bn=512,
                         rope_cols=(0, (C_HEADS + C_KV_HEADS) * HEAD_DIM),
                         half=HEAD_DIM // ROPE_FRACTION // 2)
            c_out = _win_attn(z.reshape(b, s, C_IN), p["c_sink"][j])
            x = _out_proj([c_out.reshape(m, D_MODEL)], [p["w_out_c"][j]], x, p["norm_mix_post"][i])
        x = _mlp(x, p["norm_mlp_pre"][i], p["w_up"][i], p["w_down"][i], p["norm_mlp_post"][i])
    return x.reshape(b, s, d)


def kernel(x_prompt, x_sample, norm_mix_pre, norm_mix_post, norm_mlp_pre, norm_mlp_post,
           w_in_ab, w_out_ab, a_vnorm, a_ws, a_bs, b_lambda, b_subnorm,
           w_in_c, w_out_c, c_sink, w_up, w_down):
    p = _prep_weights(dict(
        norm_mix_pre=norm_mix_pre, norm_mix_post=norm_mix_post,
        norm_mlp_pre=norm_mlp_pre, norm_mlp_post=norm_mlp_post,
        w_in_ab=w_in_ab, w_out_ab=w_out_ab, a_vnorm=a_vnorm, a_ws=a_ws, a_bs=a_bs,
        b_lambda=b_lambda, b_subnorm=b_subnorm, w_in_c=w_in_c, w_out_c=w_out_c,
        c_sink=c_sink, w_up=w_up, w_down=w_down))
    return (_trunk(x_prompt, p), _trunk(x_sample, p))
```

```python
import functools
import math

import jax
import jax.numpy as jnp
from jax import lax
from jax.experimental import pallas as pl
from jax.experimental.pallas import tpu as pltpu

F32 = jnp.float32
BF16 = jnp.bfloat16

D_MODEL = 2048
DEPTH = 4
HEAD_DIM = 128
CHUNK = 128
A_GROUPS = 8
A_WIDTH = 1024
B_HEADS = 8
B_QK_DIM = 64
B_QK_WIDTH = 1024
B_WIDTH = 1024
AB_IN = 5120
C_HEADS = 16
C_KV_HEADS = 4
C_GROUP = 4
C_WINDOW = 128
C_IN = 3072
D_FF = 8192
ROPE_THETA = 500000.0
ROPE_FRACTION = 4
EPS = 1e-6
LOG2E = 1.4426950408889634
NEG_BIG = -1e30

LANES = 128
VMEM_LIMIT = 56 * 1024 * 1024


def _params(sem, limit=VMEM_LIMIT):
    return pltpu.CompilerParams(dimension_semantics=sem, vmem_limit_bytes=limit)


def _rms(x, g):
    return x * lax.rsqrt(jnp.mean(x * x, axis=-1, keepdims=True) + EPS) * g


def _in_proj_kernel(x_ref, g_ref, w_ref, cos_ref, sa_ref, sb_ref, o_ref, h_ref, *,
                    rope_lo, rope_hi, half, bn):
    j = pl.program_id(1)

    @pl.when(j == 0)
    def _():
        h_ref[...] = _rms(x_ref[...], g_ref[...]).astype(BF16)

    is_rope = jnp.logical_and(j >= rope_lo, j < rope_hi)

    @pl.when(is_rope)
    def _():
        z = jnp.dot(h_ref[...], w_ref[...], preferred_element_type=F32)
        cos = cos_ref[...]
        sa = sa_ref[...]
        sb = sb_ref[...]
        for c in range(bn // LANES):
            zc = z[:, c * LANES:(c + 1) * LANES]
            zr = (zc * cos + pltpu.roll(zc, half, 1) * sa
                  + pltpu.roll(zc, LANES - half, 1) * sb)
            o_ref[:, c * LANES:(c + 1) * LANES] = zr.astype(BF16)

    @pl.when(jnp.logical_not(is_rope))
    def _():
        o_ref[...] = jnp.dot(h_ref[...], w_ref[...], preferred_element_type=F32).astype(BF16)


def _in_proj(x, g, w, tables, seq, *, bn, rope_cols, half, bm=512):
    m, d = x.shape
    n = w.shape[1]
    bm = min(bm, seq)
    nsb = seq // bm
    cos, sa, sb = tables
    kern = functools.partial(_in_proj_kernel, rope_lo=rope_cols[0] // bn,
                             rope_hi=rope_cols[1] // bn, half=half, bn=bn)
    tab_spec = pl.BlockSpec((bm, LANES), lambda i, j: (i % nsb, 0))
    return pl.pallas_call(
        kern,
        grid=(m // bm, n // bn),
        in_specs=[
            pl.BlockSpec((bm, d), lambda i, j: (i, 0)),
            pl.BlockSpec((1, d), lambda i, j: (0, 0)),
            pl.BlockSpec((d, bn), lambda i, j: (0, j)),
            tab_spec, tab_spec, tab_spec,
        ],
        out_specs=pl.BlockSpec((bm, bn), lambda i, j: (i, j)),
        out_shape=jax.ShapeDtypeStruct((m, n), BF16),
        scratch_shapes=[pltpu.VMEM((bm, d), BF16)],
        compiler_params=_params(("parallel", "arbitrary")),
        name="in_proj",
    )(x, g, w, cos, sa, sb)


def _rope_tables(seq, head_dim, rot):
    half = rot // 2
    inv = ROPE_THETA ** (-jnp.arange(half, dtype=F32) / half)
    ang = jnp.arange(seq, dtype=F32)[:, None] * inv[None, :]
    c, s = jnp.cos(ang), jnp.sin(ang)
    z_half = jnp.zeros((seq, half), F32)
    z_rest = jnp.zeros((seq, head_dim - rot), F32)
    cos = jnp.concatenate([c, c, jnp.ones((seq, head_dim - rot), F32)], axis=1)
    sa = jnp.concatenate([z_half, s, z_rest], axis=1)
    sb = jnp.concatenate([-s, z_half, z_rest], axis=1)
    reps = LANES // head_dim
    return tuple(jnp.tile(t, (1, reps)) for t in (cos, sa, sb))


def _gmlp_kernel(u_ref, v_ref, vn_ref, ws_ref, bs_ref, o_ref, *, tm):
    for g in range(A_GROUPS):
        cols = slice(g * HEAD_DIM, (g + 1) * HEAD_DIM)
        u = jax.nn.gelu(u_ref[:, cols].astype(F32))
        v = jax.nn.gelu(v_ref[:, cols].astype(F32))
        v = _rms(v, vn_ref[g:g + 1, :]).astype(BF16)
        w = ws_ref[g]
        b = bs_ref[g]
        for c in range(tm // CHUNK):
            rows = slice(c * CHUNK, (c + 1) * CHUNK)
            mixed = jnp.dot(w, v[rows, :], preferred_element_type=F32) + b
            o_ref[rows, cols] = (u[rows, :] * mixed).astype(BF16)


def _gmlp(z, vnorm, ws, bs_b, *, tm=512):
    m = z.shape[0]
    tm = min(tm, m)
    return pl.pallas_call(
        functools.partial(_gmlp_kernel, tm=tm),
        grid=(m // tm,),
        in_specs=[
            pl.BlockSpec((tm, A_WIDTH), lambda i: (i, 0)),
            pl.BlockSpec((tm, A_WIDTH), lambda i: (i, 1)),
            pl.BlockSpec((A_GROUPS, HEAD_DIM), lambda i: (0, 0)),
            pl.BlockSpec((A_GROUPS, CHUNK, CHUNK), lambda i: (0, 0, 0)),
            pl.BlockSpec((A_GROUPS, CHUNK, HEAD_DIM), lambda i: (0, 0, 0)),
        ],
        out_specs=pl.BlockSpec((tm, A_WIDTH), lambda i: (i, 0)),
        out_shape=jax.ShapeDtypeStruct((m, A_WIDTH), BF16),
        compiler_params=_params(("parallel",)),
        name="gmlp",
    )(z, z, vnorm, ws, bs_b)


BF16_SUBLANES = 16
VT_ROWS = HEAD_DIM + BF16_SUBLANES


def _vt_kernel(v_ref, o_ref, *, tk):
    row = lax.broadcasted_iota(jnp.int32, (BF16_SUBLANES, tk), 0)
    ones_row = jnp.where(row == 0, 1.0, 0.0).astype(BF16)
    for h in range(B_HEADS):
        v = v_ref[:, h * HEAD_DIM:(h + 1) * HEAD_DIM].astype(F32)
        o_ref[h, :HEAD_DIM, :] = v.T.astype(BF16)
        o_ref[h, HEAD_DIM:, :] = ones_row


def _v_transposed(z3, *, tk):
    b, s, _ = z3.shape
    v_block = (2 * A_WIDTH + 2 * B_QK_WIDTH) // B_WIDTH
    return pl.pallas_call(
        functools.partial(_vt_kernel, tk=tk),
        grid=(b, s // tk),
        in_specs=[pl.BlockSpec((None, tk, B_WIDTH), lambda i, c: (i, c, v_block))],
        out_specs=pl.BlockSpec((None, B_HEADS, None, VT_ROWS, tk), lambda i, c: (i, 0, c, 0, 0)),
        out_shape=jax.ShapeDtypeStruct((b, B_HEADS, s // tk, VT_ROWS, tk), BF16),
        compiler_params=_params(("parallel", "parallel")),
        name="v_transpose",
    )(z3)


def _diff_attn_kernel(q_ref, k_ref, vt_ref, lam_ref, g_ref, o_ref,
                      qt_ref, s_ref, p_ref, a_ref, acc_ref, *, tq, tk, nck, lam_init):
    q = q_ref[...].astype(F32) * (B_QK_DIM ** -0.5 * LOG2E)
    qt = q.T
    row = lax.broadcasted_iota(jnp.int32, qt.shape, 0)
    qt_ref[:, :tq] = jnp.where(row < B_QK_DIM, qt, 0.0).astype(BF16)
    qt_ref[:, tq:] = jnp.where(row >= B_QK_DIM, qt, 0.0).astype(BF16)

    def scores(c):
        start = c * tk if isinstance(c, int) else pl.multiple_of(c * tk, tk)
        return jnp.dot(k_ref[pl.ds(start, tk), :], qt_ref[...], preferred_element_type=F32)

    def softmax(s, m_old):
        m_new = jnp.maximum(m_old, jnp.max(s, axis=0, keepdims=True))
        alpha = jnp.exp2(m_old - m_new)
        p = jnp.exp2((s - m_new).astype(BF16))
        return p, alpha, m_new

    def accumulate(c, p, alpha):
        acc_ref[...] = alpha * acc_ref[...] + jnp.dot(vt_ref[c], p, preferred_element_type=F32)

    def chunk_pair(j, m, first, last):
        c0 = 2 * j
        s1 = scores(c0 + 1)
        p0, a0, m = softmax(s_ref[...], m)
        if not first:
            accumulate(c0 - 1, p_ref[...], a_ref[...])
        if not last:
            s_ref[...] = scores(c0 + 2)
        p1, a1, m = softmax(s1, m)
        accumulate(c0, p0, a0)
        p_ref[...] = p1
        a_ref[...] = a1
        return m

    s_ref[...] = scores(0)
    acc_ref[...] = jnp.zeros(acc_ref.shape, F32)
    m = jnp.full((1, 2 * tq), NEG_BIG, F32)
    n_pairs = nck // 2
    if n_pairs == 1:
        m = chunk_pair(0, m, True, True)
    else:
        m = chunk_pair(0, m, True, False)
        m = lax.fori_loop(1, n_pairs - 1, lambda j, mm: chunk_pair(j, mm, False, False), m)
        m = chunk_pair(n_pairs - 1, m, False, True)
    accumulate(nck - 1, p_ref[...], a_ref[...])

    lv = lam_ref[...]
    lam = (jnp.exp(jnp.sum(lv[0:1] * lv[1:2], axis=1, keepdims=True))
           - jnp.exp(jnp.sum(lv[2:3] * lv[3:4], axis=1, keepdims=True)) + lam_init)
    inv_l = 1.0 / acc_ref[HEAD_DIM:HEAD_DIM + 1, :]
    acc = acc_ref[:HEAD_DIM, :] * inv_l
    ot = acc[:, :tq] - lam * acc[:, tq:]
    o = ot.T
    o_ref[...] = (_rms(o, g_ref[...]) * (1.0 - lam_init)).astype(BF16)


def _diff_attn(z3, vt, lam_vecs, subnorm, lam_init, *, tq, tk):
    b, s, _ = z3.shape
    nck = s // tk
    q_block = 2 * A_WIDTH // LANES
    k_block = (2 * A_WIDTH + B_QK_WIDTH) // LANES
    kern = functools.partial(_diff_attn_kernel, tq=tq, tk=tk, nck=nck, lam_init=lam_init)
    return pl.pallas_call(
        kern,
        grid=(b, B_HEADS, s // tq),
        in_specs=[
            pl.BlockSpec((None, tq, LANES), lambda i, h, t: (i, t, q_block + h)),
            pl.BlockSpec((None, s, LANES), lambda i, h, t: (i, 0, k_block + h)),
            pl.BlockSpec((None, None, nck, VT_ROWS, tk), lambda i, h, t: (i, h, 0, 0, 0)),
            pl.BlockSpec((4, B_QK_DIM), lambda i, h, t: (0, 0)),
            pl.BlockSpec((1, HEAD_DIM), lambda i, h, t: (0, 0)),
        ],
        out_specs=pl.BlockSpec((None, tq, HEAD_DIM), lambda i, h, t: (i, t, h)),
        out_shape=jax.ShapeDtypeStruct((b, s, B_WIDTH), BF16),
        scratch_shapes=[
            pltpu.VMEM((HEAD_DIM, 2 * tq), BF16),
            pltpu.VMEM((tk, 2 * tq), F32),
            pltpu.VMEM((tk, 2 * tq), BF16),
            pltpu.VMEM((1, 2 * tq), F32),
            pltpu.VMEM((VT_ROWS, 2 * tq), F32),
        ],
        compiler_params=_params(("parallel", "parallel", "arbitrary")),
        name="diff_attn",
    )(z3, z3, vt, lam_vecs, subnorm)


def _win_attn_kernel(sink_ref, q_ref, kp_ref, kc_ref, kn_ref, vp_ref, vc_ref, vn_ref, o_ref, *, nb):
    n = pl.program_id(1)
    rows = C_GROUP * CHUNK
    r = lax.broadcasted_iota(jnp.int32, (rows, 3 * CHUNK), 0) % CHUNK
    c = lax.broadcasted_iota(jnp.int32, (rows, 3 * CHUNK), 1)
    rel = c - CHUNK - r
    valid = jnp.logical_and(rel <= C_WINDOW, rel >= -C_WINDOW)
    valid = jnp.logical_and(valid, jnp.logical_or(n > 0, c >= CHUNK))
    valid = jnp.logical_and(valid, jnp.logical_or(n < nb - 1, c < 2 * CHUNK))
    lane = lax.broadcasted_iota(jnp.int32, (3 * CHUNK, LANES), 1)
    ones_col = jnp.where(lane == 0, 1.0, 0.0).astype(BF16)
    scale = HEAD_DIM ** -0.5 * LOG2E
    for kh in range(C_KV_HEADS):
        kcols = slice(kh * HEAD_DIM, (kh + 1) * HEAD_DIM)
        kband = jnp.concatenate([kp_ref[:, kcols], kc_ref[:, kcols], kn_ref[:, kcols]], axis=0)
        vband = jnp.concatenate([vp_ref[:, kcols], vc_ref[:, kcols], vn_ref[:, kcols]], axis=0)
        v_aug = jnp.concatenate([vband, ones_col], axis=1)
        heads = [kh * C_GROUP + g for g in range(C_GROUP)]
        q = jnp.concatenate([q_ref[:, h * HEAD_DIM:(h + 1) * HEAD_DIM] for h in heads], axis=0)
        q = (q.astype(F32) * scale).astype(BF16)
        sink = jnp.concatenate([jnp.full((CHUNK, 1), sink_ref[h] * LOG2E, F32) for h in heads], axis=0)
        s = lax.dot_general(q, kband, (((1,), (1,)), ((), ())), preferred_element_type=F32)
        s = jnp.where(valid, s, NEG_BIG)
        m = jnp.maximum(jnp.max(s, axis=-1, keepdims=True), sink)
        p = jnp.exp2((s - m).astype(BF16))
        o_aug = jnp.dot(p, v_aug, preferred_element_type=F32)
        denom = o_aug[:, HEAD_DIM:HEAD_DIM + 1] + jnp.exp2(sink - m)
        o = o_aug[:, :HEAD_DIM] / denom
        for g, h in enumerate(heads):
            o_ref[:, h * HEAD_DIM:(h + 1) * HEAD_DIM] = o[g * CHUNK:(g + 1) * CHUNK, :].astype(BF16)


def _win_attn(z3, sink):
    b, s, _ = z3.shape
    nb = s // CHUNK
    kv_w = C_KV_HEADS * HEAD_DIM
    k_block = C_HEADS * HEAD_DIM // kv_w
    v_block = k_block + 1
    prev = lambda n: jnp.maximum(n - 1, 0)
    nxt = lambda n: jnp.minimum(n + 1, nb - 1)

    def kv_spec(blk, f):
        return pl.BlockSpec((None, CHUNK, kv_w), lambda i, n, sk: (i, f(n), blk))

    same = lambda n: n
    grid_spec = pltpu.PrefetchScalarGridSpec(
        num_scalar_prefetch=1,
        grid=(b, nb),
        in_specs=[
            pl.BlockSpec((None, CHUNK, C_HEADS * HEAD_DIM), lambda i, n, sk: (i, n, 0)),
            kv_spec(k_block, prev), kv_spec(k_block, same), kv_spec(k_block, nxt),
            kv_spec(v_block, prev), kv_spec(v_block, same), kv_spec(v_block, nxt),
        ],
        out_specs=pl.BlockSpec((None, CHUNK, C_HEADS * HEAD_DIM), lambda i, n, sk: (i, n, 0)),
    )
    return pl.pallas_call(
        functools.partial(_win_attn_kernel, nb=nb),
        grid_spec=grid_spec,
        out_shape=jax.ShapeDtypeStruct((b, s, C_HEADS * HEAD_DIM), BF16),
        compiler_params=_params(("parallel", "parallel")),
        name="win_attn",
    )(sink, z3, z3, z3, z3, z3, z3, z3)


def _out_proj_kernel(*refs, n_in):
    a_refs = refs[:n_in]
    w_refs = refs[n_in:2 * n_in]
    x_ref, g_ref, o_ref = refs[2 * n_in:]
    acc = jnp.dot(a_refs[0][...], w_refs[0][...], preferred_element_type=F32)
    for a_ref, w_ref in zip(a_refs[1:], w_refs[1:]):
        acc = acc + jnp.dot(a_ref[...], w_ref[...], preferred_element_type=F32)
    o_ref[...] = x_ref[...] + _rms(acc, g_ref[...])


def _out_proj(acts, ws, x, g, *, bm=512):
    m, d = x.shape
    bm = min(bm, m)
    n_in = len(acts)
    in_specs = [pl.BlockSpec((bm, a.shape[1]), lambda i: (i, 0)) for a in acts]
    in_specs += [pl.BlockSpec(w.shape, lambda i: (0, 0)) for w in ws]
    in_specs += [pl.BlockSpec((bm, d), lambda i: (i, 0)), pl.BlockSpec((1, d), lambda i: (0, 0))]
    return pl.pallas_call(
        functools.partial(_out_proj_kernel, n_in=n_in),
        grid=(m // bm,),
        in_specs=in_specs,
        out_specs=pl.BlockSpec((bm, d), lambda i: (i, 0)),
        out_shape=jax.ShapeDtypeStruct((m, d), F32),
        compiler_params=_params(("parallel",)),
        name="out_proj",
    )(*acts, *ws, x, g)


def _mlp_kernel(x_ref, g1_ref, wu_ref, wd_ref, g2_ref, o_ref, h_ref, acc_ref):
    f = pl.program_id(1)

    @pl.when(f == 0)
    def _():
        h_ref[...] = _rms(x_ref[...], g1_ref[...]).astype(BF16)
        acc_ref[...] = jnp.zeros(acc_ref.shape, F32)

    a = jnp.dot(h_ref[...], wu_ref[...], preferred_element_type=F32)
    a = jnp.square(jnp.maximum(a, 0.0)).astype(BF16)
    acc_ref[...] += jnp.dot(a, wd_ref[...], preferred_element_type=F32)

    @pl.when(f == pl.num_programs(1) - 1)
    def _():
        o_ref[...] = x_ref[...] + _rms(acc_ref[...], g2_ref[...])


def _mlp(x, g1, wu, wd, g2, *, bm=512, tf=1024):
    m, d = x.shape
    dff = wu.shape[1]
    bm = min(bm, m)
    return pl.pallas_call(
        _mlp_kernel,
        grid=(m // bm, dff // tf),
        in_specs=[
            pl.BlockSpec((bm, d), lambda i, f: (i, 0)),
            pl.BlockSpec((1, d), lambda i, f: (0, 0)),
            pl.BlockSpec((d, tf), lambda i, f: (0, f)),
            pl.BlockSpec((tf, d), lambda i, f: (f, 0)),
            pl.BlockSpec((1, d), lambda i, f: (0, 0)),
        ],
        out_specs=pl.BlockSpec((bm, d), lambda i, f: (i, 0)),
        out_shape=jax.ShapeDtypeStruct((m, d), F32),
        scratch_shapes=[pltpu.VMEM((bm, d), BF16), pltpu.VMEM((bm, d), F32)],
        compiler_params=_params(("parallel", "arbitrary")),
        name="mlp",
    )(x, g1, wu, wd, g2)


def _pair_heads(w):
    d = w.shape[0]
    return w.reshape(d, 2, B_HEADS, B_QK_DIM).transpose(0, 2, 1, 3).reshape(d, B_QK_WIDTH)


def _prep_weights(p):
    row = lambda a: a.reshape(a.shape[0], 1, a.shape[-1]).astype(F32)
    w_in_ab = p["w_in_ab"]
    q0, k0, v0 = 2 * A_WIDTH, 2 * A_WIDTH + B_QK_WIDTH, 2 * A_WIDTH + 2 * B_QK_WIDTH
    w_in_ab = jnp.concatenate([
        w_in_ab[:, :, :q0],
        jax.vmap(_pair_heads)(w_in_ab[:, :, q0:k0]),
        jax.vmap(_pair_heads)(w_in_ab[:, :, k0:v0]),
        w_in_ab[:, :, v0:],
    ], axis=-1).astype(BF16)
    return dict(
        norm_mix_pre=row(p["norm_mix_pre"]), norm_mix_post=row(p["norm_mix_post"]),
        norm_mlp_pre=row(p["norm_mlp_pre"]), norm_mlp_post=row(p["norm_mlp_post"]),
        w_in_ab=w_in_ab,
        w_out_ab=p["w_out_ab"].astype(BF16),
        a_vnorm=p["a_vnorm"].astype(F32),
        a_ws=p["a_ws"].astype(BF16),
        a_bs_b=jnp.broadcast_to(p["a_bs"].astype(F32)[..., None],
                                p["a_bs"].shape + (HEAD_DIM,)),
        b_lambda=p["b_lambda"].astype(F32),
        b_subnorm=p["b_subnorm"].astype(F32)[:, None, :],
        w_in_c=p["w_in_c"].astype(BF16),
        w_out_c=p["w_out_c"].astype(BF16),
        c_sink=p["c_sink"].astype(F32),
        w_up=p["w_up"].astype(BF16),
        w_down=p["w_down"].astype(BF16),
    )


def _trunk(x3, p):
    b, s, d = x3.shape
    m = b * s
    x = x3.reshape(m, d)
    tq = 512 if s >= 8192 else min(256, s)
    tk = 1024 if s >= 8192 else min(512, s // 2)
    tab_ab = _rope_tables(s, B_QK_DIM, B_QK_DIM // ROPE_FRACTION)
    tab_c = _rope_tables(s, HEAD_DIM, HEAD_DIM // ROPE_FRACTION)
    for i in range(DEPTH):
        j = i // 2
        if i % 2 == 0:
            lam_init = 0.8 - 0.6 * math.exp(-0.3 * i)
            z = _in_proj(x, p["norm_mix_pre"][i], p["w_in_ab"][j], tab_ab, s, bn=1024,
                         rope_cols=(2 * A_WIDTH, 2 * A_WIDTH + 2 * B_QK_WIDTH),
                         half=B_QK_DIM // ROPE_FRACTION // 2)
            a_out = _gmlp(z, p["a_vnorm"][j], p["a_ws"][j], p["a_bs_b"][j])
            z3 = z.reshape(b, s, AB_IN)
            vt = _v_transposed(z3, tk=tk)
            b_out = _diff_attn(z3, vt, p["b_lambda"][j], p["b_subnorm"][j], lam_init, tq=tq, tk=tk)
            w_out = p["w_out_ab"][j]
            x = _out_proj([a_out, b_out.reshape(m, B_WIDTH)], [w_out[:A_WIDTH], w_out[A_WIDTH:]],
                          x, p["norm_mix_post"][i])
        else:
            z = _in_proj(x, p["norm_mix_pre"][i], p["w_in_c"][j], tab_c, s, bn=512,
                         rope_cols=(0, (C_HEADS + C_KV_HEADS) * HEAD_DIM),
                         half=HEAD_DIM // ROPE_FRACTION // 2)
            c_out = _win_attn(z.reshape(b, s, C_IN), p["c_sink"][j])
            x = _out_proj([c_out.reshape(m, D_MODEL)], [p["w_out_c"][j]], x, p["norm_mix_post"][i])
        x = _mlp(x, p["norm_mlp_pre"][i], p["w_up"][i], p["w_down"][i], p["norm_mlp_post"][i])
    return x.reshape(b, s, d)


def kernel(x_prompt, x_sample, norm_mix_pre, norm_mix_post, norm_mlp_pre, norm_mlp_post,
           w_in_ab, w_out_ab, a_vnorm, a_ws, a_bs, b_lambda, b_subnorm,
           w_in_c, w_out_c, c_sink, w_up, w_down):
    p = _prep_weights(dict(
        norm_mix_pre=norm_mix_pre, norm_mix_post=norm_mix_post,
        norm_mlp_pre=norm_mlp_pre, norm_mlp_post=norm_mlp_post,
        w_in_ab=w_in_ab, w_out_ab=w_out_ab, a_vnorm=a_vnorm, a_ws=a_ws, a_bs=a_bs,
        b_lambda=b_lambda, b_subnorm=b_subnorm, w_in_c=w_in_c, w_out_c=w_out_c,
        c_sink=c_sink, w_up=w_up, w_down=w_down))
    return (_trunk(x_prompt, p), _trunk(x_sample, p))
```

```python
import functools
import math

import jax
import jax.numpy as jnp
from jax import lax
from jax.experimental import pallas as pl
from jax.experimental.pallas import tpu as pltpu

F32 = jnp.float32
BF16 = jnp.bfloat16

D_MODEL = 2048
DEPTH = 4
HEAD_DIM = 128
CHUNK = 128
A_GROUPS = 8
A_WIDTH = 1024
B_HEADS = 8
B_QK_DIM = 64
B_QK_WIDTH = 1024
B_WIDTH = 1024
AB_IN = 5120
C_HEADS = 16
C_KV_HEADS = 4
C_GROUP = 4
C_WINDOW = 128
C_IN = 3072
D_FF = 8192
ROPE_THETA = 500000.0
ROPE_FRACTION = 4
EPS = 1e-6
LOG2E = 1.4426950408889634
NEG_BIG = -1e30

LANES = 128
VMEM_LIMIT = 56 * 1024 * 1024


def _params(sem, limit=VMEM_LIMIT, flags=None):
    return pltpu.CompilerParams(dimension_semantics=sem, vmem_limit_bytes=limit, flags=flags)


def _rms(x, g):
    return x * lax.rsqrt(jnp.mean(x * x, axis=-1, keepdims=True) + EPS) * g


def _in_proj_kernel(x_ref, g_ref, w_ref, cos_ref, sa_ref, sb_ref, o_ref, h_ref, *,
                    rope_lo, rope_hi, half, bn):
    j = pl.program_id(1)

    @pl.when(j == 0)
    def _():
        h_ref[...] = _rms(x_ref[...], g_ref[...]).astype(BF16)

    is_rope = jnp.logical_and(j >= rope_lo, j < rope_hi)

    @pl.when(is_rope)
    def _():
        z = jnp.dot(h_ref[...], w_ref[...], preferred_element_type=F32)
        cos = cos_ref[...]
        sa = sa_ref[...]
        sb = sb_ref[...]
        for c in range(bn // LANES):
            zc = z[:, c * LANES:(c + 1) * LANES]
            zr = (zc * cos + pltpu.roll(zc, half, 1) * sa
                  + pltpu.roll(zc, LANES - half, 1) * sb)
            o_ref[:, c * LANES:(c + 1) * LANES] = zr.astype(BF16)

    @pl.when(jnp.logical_not(is_rope))
    def _():
        o_ref[...] = jnp.dot(h_ref[...], w_ref[...], preferred_element_type=F32).astype(BF16)


def _in_proj(x, g, w, tables, seq, *, bn, rope_cols, half, bm=1024):
    m, d = x.shape
    n = w.shape[1]
    bm = min(bm, seq)
    nsb = seq // bm
    cos, sa, sb = tables
    kern = functools.partial(_in_proj_kernel, rope_lo=rope_cols[0] // bn,
                             rope_hi=rope_cols[1] // bn, half=half, bn=bn)
    tab_spec = pl.BlockSpec((bm, LANES), lambda i, j: (i % nsb, 0))
    return pl.pallas_call(
        kern,
        grid=(m // bm, n // bn),
        in_specs=[
            pl.BlockSpec((bm, d), lambda i, j: (i, 0)),
            pl.BlockSpec((1, d), lambda i, j: (0, 0)),
            pl.BlockSpec((d, bn), lambda i, j: (0, j)),
            tab_spec, tab_spec, tab_spec,
        ],
        out_specs=pl.BlockSpec((bm, bn), lambda i, j: (i, j)),
        out_shape=jax.ShapeDtypeStruct((m, n), BF16),
        scratch_shapes=[pltpu.VMEM((bm, d), BF16)],
        compiler_params=_params(("parallel", "arbitrary")),
        name="in_proj",
    )(x, g, w, cos, sa, sb)


def _rope_tables(seq, head_dim, rot):
    half = rot // 2
    inv = ROPE_THETA ** (-jnp.arange(half, dtype=F32) / half)
    ang = jnp.arange(seq, dtype=F32)[:, None] * inv[None, :]
    c, s = jnp.cos(ang), jnp.sin(ang)
    z_half = jnp.zeros((seq, half), F32)
    z_rest = jnp.zeros((seq, head_dim - rot), F32)
    cos = jnp.concatenate([c, c, jnp.ones((seq, head_dim - rot), F32)], axis=1)
    sa = jnp.concatenate([z_half, s, z_rest], axis=1)
    sb = jnp.concatenate([-s, z_half, z_rest], axis=1)
    reps = LANES // head_dim
    return tuple(jnp.tile(t, (1, reps)) for t in (cos, sa, sb))


def _gmlp_kernel(u_ref, v_ref, vn_ref, ws_ref, bs_ref, o_ref, *, tm):
    for g in range(A_GROUPS):
        cols = slice(g * HEAD_DIM, (g + 1) * HEAD_DIM)
        u = jax.nn.gelu(u_ref[:, cols].astype(F32))
        v = jax.nn.gelu(v_ref[:, cols].astype(F32))
        v = _rms(v, vn_ref[g:g + 1, :]).astype(BF16)
        w = ws_ref[g]
        b = bs_ref[g]
        for c in range(tm // CHUNK):
            rows = slice(c * CHUNK, (c + 1) * CHUNK)
            mixed = jnp.dot(w, v[rows, :], preferred_element_type=F32) + b
            o_ref[rows, cols] = (u[rows, :] * mixed).astype(BF16)


def _gmlp(z, vnorm, ws, bs_b, *, tm=512):
    m = z.shape[0]
    tm = min(tm, m)
    return pl.pallas_call(
        functools.partial(_gmlp_kernel, tm=tm),
        grid=(m // tm,),
        in_specs=[
            pl.BlockSpec((tm, A_WIDTH), lambda i: (i, 0)),
            pl.BlockSpec((tm, A_WIDTH), lambda i: (i, 1)),
            pl.BlockSpec((A_GROUPS, HEAD_DIM), lambda i: (0, 0)),
            pl.BlockSpec((A_GROUPS, CHUNK, CHUNK), lambda i: (0, 0, 0)),
            pl.BlockSpec((A_GROUPS, CHUNK, HEAD_DIM), lambda i: (0, 0, 0)),
        ],
        out_specs=pl.BlockSpec((tm, A_WIDTH), lambda i: (i, 0)),
        out_shape=jax.ShapeDtypeStruct((m, A_WIDTH), BF16),
        compiler_params=_params(("parallel",)),
        name="gmlp",
    )(z, z, vnorm, ws, bs_b)


BF16_SUBLANES = 16
VT_ROWS = HEAD_DIM + BF16_SUBLANES


def _vt_kernel(v_ref, o_ref, *, tk):
    row = lax.broadcasted_iota(jnp.int32, (BF16_SUBLANES, tk), 0)
    ones_row = jnp.where(row == 0, 1.0, 0.0).astype(BF16)
    for h in range(B_HEADS):
        v = v_ref[:, h * HEAD_DIM:(h + 1) * HEAD_DIM].astype(F32)
        o_ref[h, :HEAD_DIM, :] = v.T.astype(BF16)
        o_ref[h, HEAD_DIM:, :] = ones_row


def _v_transposed(z3, *, tk):
    b, s, _ = z3.shape
    v_block = (2 * A_WIDTH + 2 * B_QK_WIDTH) // B_WIDTH
    return pl.pallas_call(
        functools.partial(_vt_kernel, tk=tk),
        grid=(b, s // tk),
        in_specs=[pl.BlockSpec((None, tk, B_WIDTH), lambda i, c: (i, c, v_block))],
        out_specs=pl.BlockSpec((None, B_HEADS, None, VT_ROWS, tk), lambda i, c: (i, 0, c, 0, 0)),
        out_shape=jax.ShapeDtypeStruct((b, B_HEADS, s // tk, VT_ROWS, tk), BF16),
        compiler_params=_params(("parallel", "parallel")),
        name="v_transpose",
    )(z3)


def _diff_attn_kernel(q_ref, k_ref, vt_ref, lam_ref, g_ref, o_ref,
                      qt_ref, s_ref, p_ref, a_ref, acc_ref, *, tq, tk, nck, lam_init):
    q = q_ref[...].astype(F32) * (B_QK_DIM ** -0.5 * LOG2E)
    qt = q.T
    row = lax.broadcasted_iota(jnp.int32, qt.shape, 0)
    qt_ref[:, :tq] = jnp.where(row < B_QK_DIM, qt, 0.0).astype(BF16)
    qt_ref[:, tq:] = jnp.where(row >= B_QK_DIM, qt, 0.0).astype(BF16)

    def scores(c):
        start = c * tk if isinstance(c, int) else pl.multiple_of(c * tk, tk)
        return jnp.dot(k_ref[pl.ds(start, tk), :], qt_ref[...], preferred_element_type=F32)

    def softmax(s, m_old):
        m_new = jnp.maximum(m_old, jnp.max(s, axis=0, keepdims=True))
        alpha = jnp.exp2(m_old - m_new)
        p = jnp.exp2((s - m_new).astype(BF16))
        return p, alpha, m_new

    def accumulate(c, p, alpha):
        acc_ref[...] = alpha * acc_ref[...] + jnp.dot(vt_ref[c], p, preferred_element_type=F32)

    def stage(c, slot, m, with_scores, with_values):
        if with_scores:
            s_ref[1 - slot] = scores(c + 1)
        p, alpha, m = softmax(s_ref[slot], m)
        if with_values:
            accumulate(c - 1, p_ref[1 - slot], a_ref[1 - slot])
        p_ref[slot] = p
        a_ref[slot] = alpha
        return m

    def two_stages(j, m):
        m = stage(2 * j + 1, 1, m, True, True)
        return stage(2 * j + 2, 0, m, True, True)

    s_ref[0] = scores(0)
    acc_ref[...] = jnp.zeros(acc_ref.shape, F32)
    m = jnp.full((1, 2 * tq), NEG_BIG, F32)
    m = stage(0, 0, m, True, False)
    m = lax.fori_loop(0, (nck - 2) // 2, two_stages, m)
    m = stage(nck - 1, 1, m, False, True)
    accumulate(nck - 1, p_ref[1], a_ref[1])

    lv = lam_ref[...]
    lam = (jnp.exp(jnp.sum(lv[0:1] * lv[1:2], axis=1, keepdims=True))
           - jnp.exp(jnp.sum(lv[2:3] * lv[3:4], axis=1, keepdims=True)) + lam_init)
    inv_l = 1.0 / acc_ref[HEAD_DIM:HEAD_DIM + 1, :]
    acc = acc_ref[:HEAD_DIM, :] * inv_l
    ot = acc[:, :tq] - lam * acc[:, tq:]
    o = ot.T
    o_ref[...] = (_rms(o, g_ref[...]) * (1.0 - lam_init)).astype(BF16)


def _diff_attn(z3, vt, lam_vecs, subnorm, lam_init, *, tq, tk):
    b, s, _ = z3.shape
    nck = s // tk
    q_block = 2 * A_WIDTH // LANES
    k_block = (2 * A_WIDTH + B_QK_WIDTH) // LANES
    kern = functools.partial(_diff_attn_kernel, tq=tq, tk=tk, nck=nck, lam_init=lam_init)
    return pl.pallas_call(
        kern,
        grid=(b, B_HEADS, s // tq),
        in_specs=[
            pl.BlockSpec((None, tq, LANES), lambda i, h, t: (i, t, q_block + h)),
            pl.BlockSpec((None, s, LANES), lambda i, h, t: (i, 0, k_block + h)),
            pl.BlockSpec((None, None, nck, VT_ROWS, tk), lambda i, h, t: (i, h, 0, 0, 0)),
            pl.BlockSpec((4, B_QK_DIM), lambda i, h, t: (0, 0)),
            pl.BlockSpec((1, HEAD_DIM), lambda i, h, t: (0, 0)),
        ],
        out_specs=pl.BlockSpec((None, tq, HEAD_DIM), lambda i, h, t: (i, t, h)),
        out_shape=jax.ShapeDtypeStruct((b, s, B_WIDTH), BF16),
        scratch_shapes=[
            pltpu.VMEM((HEAD_DIM, 2 * tq), BF16),
            pltpu.VMEM((2, tk, 2 * tq), F32),
            pltpu.VMEM((2, tk, 2 * tq), BF16),
            pltpu.VMEM((2, 1, 2 * tq), F32),
            pltpu.VMEM((VT_ROWS, 2 * tq), F32),
        ],
        compiler_params=_params(("parallel", "parallel", "arbitrary")),
        name="diff_attn",
    )(z3, z3, vt, lam_vecs, subnorm)


def _win_attn_kernel(sink_ref, q_ref, kp_ref, kc_ref, kn_ref, vp_ref, vc_ref, vn_ref, o_ref, *, nb):
    n = pl.program_id(1)
    rows = C_GROUP * CHUNK
    r = lax.broadcasted_iota(jnp.int32, (rows, 3 * CHUNK), 0) % CHUNK
    c = lax.broadcasted_iota(jnp.int32, (rows, 3 * CHUNK), 1)
    rel = c - CHUNK - r
    valid = jnp.logical_and(rel <= C_WINDOW, rel >= -C_WINDOW)
    valid = jnp.logical_and(valid, jnp.logical_or(n > 0, c >= CHUNK))
    valid = jnp.logical_and(valid, jnp.logical_or(n < nb - 1, c < 2 * CHUNK))
    lane = lax.broadcasted_iota(jnp.int32, (3 * CHUNK, LANES), 1)
    ones_col = jnp.where(lane == 0, 1.0, 0.0).astype(BF16)
    scale = HEAD_DIM ** -0.5 * LOG2E
    for kh in range(C_KV_HEADS):
        kcols = slice(kh * HEAD_DIM, (kh + 1) * HEAD_DIM)
        kband = jnp.concatenate([kp_ref[:, kcols], kc_ref[:, kcols], kn_ref[:, kcols]], axis=0)
        vband = jnp.concatenate([vp_ref[:, kcols], vc_ref[:, kcols], vn_ref[:, kcols]], axis=0)
        v_aug = jnp.concatenate([vband, ones_col], axis=1)
        heads = [kh * C_GROUP + g for g in range(C_GROUP)]
        q = jnp.concatenate([q_ref[:, h * HEAD_DIM:(h + 1) * HEAD_DIM] for h in heads], axis=0)
        q = (q.astype(F32) * scale).astype(BF16)
        sink = jnp.concatenate([jnp.full((CHUNK, 1), sink_ref[h] * LOG2E, F32) for h in heads], axis=0)
        s = lax.dot_general(q, kband, (((1,), (1,)), ((), ())), preferred_element_type=F32)
        s = jnp.where(valid, s, NEG_BIG)
        m = jnp.maximum(jnp.max(s, axis=-1, keepdims=True), sink)
        p = jnp.exp2((s - m).astype(BF16))
        o_aug = jnp.dot(p, v_aug, preferred_element_type=F32)
        denom = o_aug[:, HEAD_DIM:HEAD_DIM + 1] + jnp.exp2(sink - m)
        o = o_aug[:, :HEAD_DIM] / denom
        for g, h in enumerate(heads):
            o_ref[:, h * HEAD_DIM:(h + 1) * HEAD_DIM] = o[g * CHUNK:(g + 1) * CHUNK, :].astype(BF16)


def _win_attn(z3, sink):
    b, s, _ = z3.shape
    nb = s // CHUNK
    kv_w = C_KV_HEADS * HEAD_DIM
    k_block = C_HEADS * HEAD_DIM // kv_w
    v_block = k_block + 1
    prev = lambda n: jnp.maximum(n - 1, 0)
    nxt = lambda n: jnp.minimum(n + 1, nb - 1)

    def kv_spec(blk, f):
        return pl.BlockSpec((None, CHUNK, kv_w), lambda i, n, sk: (i, f(n), blk))

    same = lambda n: n
    grid_spec = pltpu.PrefetchScalarGridSpec(
        num_scalar_prefetch=1,
        grid=(b, nb),
        in_specs=[
            pl.BlockSpec((None, CHUNK, C_HEADS * HEAD_DIM), lambda i, n, sk: (i, n, 0)),
            kv_spec(k_block, prev), kv_spec(k_block, same), kv_spec(k_block, nxt),
            kv_spec(v_block, prev), kv_spec(v_block, same), kv_spec(v_block, nxt),
        ],
        out_specs=pl.BlockSpec((None, CHUNK, C_HEADS * HEAD_DIM), lambda i, n, sk: (i, n, 0)),
    )
    return pl.pallas_call(
        functools.partial(_win_attn_kernel, nb=nb),
        grid_spec=grid_spec,
        out_shape=jax.ShapeDtypeStruct((b, s, C_HEADS * HEAD_DIM), BF16),
        compiler_params=_params(("parallel", "parallel")),
        name="win_attn",
    )(sink, z3, z3, z3, z3, z3, z3, z3)


def _out_proj_kernel(*refs, n_in):
    a_refs = refs[:n_in]
    w_refs = refs[n_in:2 * n_in]
    x_ref, g_ref, o_ref = refs[2 * n_in:]
    acc = jnp.dot(a_refs[0][...], w_refs[0][...], preferred_element_type=F32)
    for a_ref, w_ref in zip(a_refs[1:], w_refs[1:]):
        acc = acc + jnp.dot(a_ref[...], w_ref[...], preferred_element_type=F32)
    o_ref[...] = x_ref[...] + _rms(acc, g_ref[...])


def _out_proj(acts, ws, x, g, *, bm=512):
    m, d = x.shape
    bm = min(bm, m)
    n_in = len(acts)
    in_specs = [pl.BlockSpec((bm, a.shape[1]), lambda i: (i, 0)) for a in acts]
    in_specs += [pl.BlockSpec(w.shape, lambda i: (0, 0)) for w in ws]
    in_specs += [pl.BlockSpec((bm, d), lambda i: (i, 0)), pl.BlockSpec((1, d), lambda i: (0, 0))]
    return pl.pallas_call(
        functools.partial(_out_proj_kernel, n_in=n_in),
        grid=(m // bm,),
        in_specs=in_specs,
        out_specs=pl.BlockSpec((bm, d), lambda i: (i, 0)),
        out_shape=jax.ShapeDtypeStruct((m, d), F32),
        compiler_params=_params(("parallel",)),
        name="out_proj",
    )(*acts, *ws, x, g)


def _mlp_kernel(x_ref, g1_ref, wu_ref, wd_ref, g2_ref, o_ref, h_ref, acc_ref):
    f = pl.program_id(1)

    @pl.when(f == 0)
    def _():
        h_ref[...] = _rms(x_ref[...], g1_ref[...]).astype(BF16)
        acc_ref[...] = jnp.zeros(acc_ref.shape, F32)

    a = jnp.dot(h_ref[...], wu_ref[...], preferred_element_type=F32)
    a = jnp.square(jnp.maximum(a, 0.0)).astype(BF16)
    acc_ref[...] += jnp.dot(a, wd_ref[...], preferred_element_type=F32)

    @pl.when(f == pl.num_programs(1) - 1)
    def _():
        o_ref[...] = x_ref[...] + _rms(acc_ref[...], g2_ref[...])


def _mlp(x, g1, wu, wd, g2, *, bm=512, tf=1024):
    m, d = x.shape
    dff = wu.shape[1]
    bm = min(bm, m)
    return pl.pallas_call(
        _mlp_kernel,
        grid=(m // bm, dff // tf),
        in_specs=[
            pl.BlockSpec((bm, d), lambda i, f: (i, 0)),
            pl.BlockSpec((1, d), lambda i, f: (0, 0)),
            pl.BlockSpec((d, tf), lambda i, f: (0, f)),
            pl.BlockSpec((tf, d), lambda i, f: (f, 0)),
            pl.BlockSpec((1, d), lambda i, f: (0, 0)),
        ],
        out_specs=pl.BlockSpec((bm, d), lambda i, f: (i, 0)),
        out_shape=jax.ShapeDtypeStruct((m, d), F32),
        scratch_shapes=[pltpu.VMEM((bm, d), BF16), pltpu.VMEM((bm, d), F32)],
        compiler_params=_params(("parallel", "arbitrary")),
        name="mlp",
    )(x, g1, wu, wd, g2)


def _pair_heads(w):
    d = w.shape[0]
    return w.reshape(d, 2, B_HEADS, B_QK_DIM).transpose(0, 2, 1, 3).reshape(d, B_QK_WIDTH)


def _prep_weights(p):
    row = lambda a: a.reshape(a.shape[0], 1, a.shape[-1]).astype(F32)
    w_in_ab = p["w_in_ab"]
    q0, k0, v0 = 2 * A_WIDTH, 2 * A_WIDTH + B_QK_WIDTH, 2 * A_WIDTH + 2 * B_QK_WIDTH
    w_in_ab = jnp.concatenate([
        w_in_ab[:, :, :q0],
        jax.vmap(_pair_heads)(w_in_ab[:, :, q0:k0]),
        jax.vmap(_pair_heads)(w_in_ab[:, :, k0:v0]),
        w_in_ab[:, :, v0:],
    ], axis=-1).astype(BF16)
    return dict(
        norm_mix_pre=row(p["norm_mix_pre"]), norm_mix_post=row(p["norm_mix_post"]),
        norm_mlp_pre=row(p["norm_mlp_pre"]), norm_mlp_post=row(p["norm_mlp_post"]),
        w_in_ab=w_in_ab,
        w_out_ab=p["w_out_ab"].astype(BF16),
        a_vnorm=p["a_vnorm"].astype(F32),
        a_ws=p["a_ws"].astype(BF16),
        a_bs_b=jnp.broadcast_to(p["a_bs"].astype(F32)[..., None],
                                p["a_bs"].shape + (HEAD_DIM,)),
        b_lambda=p["b_lambda"].astype(F32),
        b_subnorm=p["b_subnorm"].astype(F32)[:, None, :],
        w_in_c=p["w_in_c"].astype(BF16),
        w_out_c=p["w_out_c"].astype(BF16),
        c_sink=p["c_sink"].astype(F32),
        w_up=p["w_up"].astype(BF16),
        w_down=p["w_down"].astype(BF16),
    )


def _trunk(x3, p):
    b, s, d = x3.shape
    m = b * s
    x = x3.reshape(m, d)
    tq = min(512, s)
    tk = 1024 if s >= 8192 else min(512, s // 2)
    tab_ab = _rope_tables(s, B_QK_DIM, B_QK_DIM // ROPE_FRACTION)
    tab_c = _rope_tables(s, HEAD_DIM, HEAD_DIM // ROPE_FRACTION)
    for i in range(DEPTH):
        j = i // 2
        if i % 2 == 0:
            lam_init = 0.8 - 0.6 * math.exp(-0.3 * i)
            z = _in_proj(x, p["norm_mix_pre"][i], p["w_in_ab"][j], tab_ab, s, bn=1024,
                         rope_cols=(2 * A_WIDTH, 2 * A_WIDTH + 2 * B_QK_WIDTH),
                         half=B_QK_DIM // ROPE_FRACTION // 2)
            a_out = _gmlp(z, p["a_vnorm"][j], p["a_ws"][j], p["a_bs_b"][j])
            z3 = z.reshape(b, s, AB_IN)
            vt = _v_transposed(z3, tk=tk)
            b_out = _diff_attn(z3, vt, p["b_lambda"][j], p["b_subnorm"][j], lam_init, tq=tq, tk=tk)
            w_out = p["w_out_ab"][j]
            x = _out_proj([a_out, b_out.reshape(m, B_WIDTH)], [w_out[:A_WIDTH], w_out[A_WIDTH:]],
                          x, p["norm_mix_post"][i])
        else:
            z = _in_proj(x, p["norm_mix_pre"][i], p["w_in_c"][j], tab_c, s, bn=512,
                         rope_cols=(0, (C_HEADS + C_KV_HEADS) * HEAD_DIM),
                         half=HEAD_DIM // ROPE_FRACTION // 2)
            c_out = _win_attn(z.reshape(b, s, C_IN), p["c_sink"][j])
            x = _out_proj([c_out.reshape(m, D_MODEL)], [p["w_out_c"][j]], x, p["norm_mix_post"][i])
        x = _mlp(x, p["norm_mlp_pre"][i], p["w_up"][i], p["w_down"][i], p["norm_mlp_post"][i])
    return x.reshape(b, s, d)


def kernel(x_prompt, x_sample, norm_mix_pre, norm_mix_post, norm_mlp_pre, norm_mlp_post,
           w_in_ab, w_out_ab, a_vnorm, a_ws, a_bs, b_lambda, b_subnorm,
           w_in_c, w_out_c, c_sink, w_up, w_down):
    p = _prep_weights(dict(
        norm_mix_pre=norm_mix_pre, norm_mix_post=norm_mix_post,
        norm_mlp_pre=norm_mlp_pre, norm_mlp_post=norm_mlp_post,
        w_in_ab=w_in_ab, w_out_ab=w_out_ab, a_vnorm=a_vnorm, a_ws=a_ws, a_bs=a_bs,
        b_lambda=b_lambda, b_subnorm=b_subnorm, w_in_c=w_in_c, w_out_c=w_out_c,
        c_sink=c_sink, w_up=w_up, w_down=w_down))
    return (_trunk(x_prompt, p), _trunk(x_sample, p))
```

```python
import functools
import math

import jax
import jax.numpy as jnp
from jax import lax
from jax.experimental import pallas as pl
from jax.experimental.pallas import tpu as pltpu

F32 = jnp.float32
BF16 = jnp.bfloat16

D_MODEL = 2048
DEPTH = 4
HEAD_DIM = 128
CHUNK = 128
A_GROUPS = 8
A_WIDTH = 1024
B_HEADS = 8
B_QK_DIM = 64
B_QK_WIDTH = 1024
B_WIDTH = 1024
AB_IN = 5120
C_HEADS = 16
C_KV_HEADS = 4
C_GROUP = 4
C_WINDOW = 128
C_IN = 3072
D_FF = 8192
ROPE_THETA = 500000.0
ROPE_FRACTION = 4
EPS = 1e-6
LOG2E = 1.4426950408889634
NEG_BIG = -1e30

LANES = 128
VMEM_LIMIT = 56 * 1024 * 1024


def _params(sem, limit=VMEM_LIMIT, flags=None):
    return pltpu.CompilerParams(dimension_semantics=sem, vmem_limit_bytes=limit, flags=flags)


def _rms(x, g):
    return x * lax.rsqrt(jnp.mean(x * x, axis=-1, keepdims=True) + EPS) * g


def _in_proj_kernel(x_ref, g_ref, w_ref, cos_ref, sa_ref, sb_ref, o_ref, h_ref, *,
                    rope_lo, rope_hi, half, bn):
    j = pl.program_id(1)
    chunks = bn // LANES

    @pl.when(j == 0)
    def _():
        h_ref[...] = _rms(x_ref[...], g_ref[...]).astype(BF16)

    def project(n_rope):
        z = jnp.dot(h_ref[...], w_ref[...], preferred_element_type=F32)
        if n_rope:
            cos = cos_ref[...]
            sa = sa_ref[...]
            sb = sb_ref[...]
        for c in range(n_rope):
            zc = z[:, c * LANES:(c + 1) * LANES]
            zr = (zc * cos + pltpu.roll(zc, half, 1) * sa
                  + pltpu.roll(zc, LANES - half, 1) * sb)
            o_ref[:, c * LANES:(c + 1) * LANES] = zr.astype(BF16)
        if n_rope < chunks:
            o_ref[:, n_rope * LANES:] = z[:, n_rope * LANES:].astype(BF16)

    first_full, end_full, tail = rope_lo // chunks, rope_hi // chunks, rope_hi % chunks
    pl.when(jnp.logical_and(j >= first_full, j < end_full))(lambda: project(chunks))
    if tail:
        pl.when(j == end_full)(lambda: project(tail))
    n_partial = end_full + (1 if tail else 0)
    pl.when(jnp.logical_or(j < first_full, j >= n_partial))(lambda: project(0))


def _in_proj(x, g, w, tables, seq, *, bn, rope_cols, half, bm=1024):
    m, d = x.shape
    n = w.shape[1]
    bm = min(bm, seq)
    nsb = seq // bm
    cos, sa, sb = tables
    assert rope_cols[0] % bn == 0 and rope_cols[1] % LANES == 0 and seq % bm == 0
    kern = functools.partial(_in_proj_kernel, rope_lo=rope_cols[0] // LANES,
                             rope_hi=rope_cols[1] // LANES, half=half, bn=bn)
    tab_spec = pl.BlockSpec((bm, LANES), lambda i, j: (i % nsb, 0))
    return pl.pallas_call(
        kern,
        grid=(m // bm, n // bn),
        in_specs=[
            pl.BlockSpec((bm, d), lambda i, j: (i, 0)),
            pl.BlockSpec((1, d), lambda i, j: (0, 0)),
            pl.BlockSpec((d, bn), lambda i, j: (0, j)),
            tab_spec, tab_spec, tab_spec,
        ],
        out_specs=pl.BlockSpec((bm, bn), lambda i, j: (i, j)),
        out_shape=jax.ShapeDtypeStruct((m, n), BF16),
        scratch_shapes=[pltpu.VMEM((bm, d), BF16)],
        compiler_params=_params(("parallel", "arbitrary")),
        name="in_proj",
    )(x, g, w, cos, sa, sb)


def _rope_tables(seq, head_dim, rot):
    half = rot // 2
    inv = ROPE_THETA ** (-jnp.arange(half, dtype=F32) / half)
    ang = jnp.arange(seq, dtype=F32)[:, None] * inv[None, :]
    c, s = jnp.cos(ang), jnp.sin(ang)
    z_half = jnp.zeros((seq, half), F32)
    z_rest = jnp.zeros((seq, head_dim - rot), F32)
    cos = jnp.concatenate([c, c, jnp.ones((seq, head_dim - rot), F32)], axis=1)
    sa = jnp.concatenate([z_half, s, z_rest], axis=1)
    sb = jnp.concatenate([-s, z_half, z_rest], axis=1)
    reps = LANES // head_dim
    return tuple(jnp.tile(t, (1, reps)) for t in (cos, sa, sb))


def _gmlp_kernel(u_ref, v_ref, vn_ref, ws_ref, bs_ref, o_ref, *, tm):
    for g in range(A_GROUPS):
        cols = slice(g * HEAD_DIM, (g + 1) * HEAD_DIM)
        u = jax.nn.gelu(u_ref[:, cols].astype(F32))
        v = jax.nn.gelu(v_ref[:, cols].astype(F32))
        v = _rms(v, vn_ref[g:g + 1, :]).astype(BF16)
        w = ws_ref[g]
        b = bs_ref[g]
        for c in range(tm // CHUNK):
            rows = slice(c * CHUNK, (c + 1) * CHUNK)
            mixed = jnp.dot(w, v[rows, :], preferred_element_type=F32) + b
            o_ref[rows, cols] = (u[rows, :] * mixed).astype(BF16)


def _gmlp(z, vnorm, ws, bs_b, *, tm=512):
    m = z.shape[0]
    tm = min(tm, m)
    return pl.pallas_call(
        functools.partial(_gmlp_kernel, tm=tm),
        grid=(m // tm,),
        in_specs=[
            pl.BlockSpec((tm, A_WIDTH), lambda i: (i, 0)),
            pl.BlockSpec((tm, A_WIDTH), lambda i: (i, 1)),
            pl.BlockSpec((A_GROUPS, HEAD_DIM), lambda i: (0, 0)),
            pl.BlockSpec((A_GROUPS, CHUNK, CHUNK), lambda i: (0, 0, 0)),
            pl.BlockSpec((A_GROUPS, CHUNK, HEAD_DIM), lambda i: (0, 0, 0)),
        ],
        out_specs=pl.BlockSpec((tm, A_WIDTH), lambda i: (i, 0)),
        out_shape=jax.ShapeDtypeStruct((m, A_WIDTH), BF16),
        compiler_params=_params(("parallel",)),
        name="gmlp",
    )(z, z, vnorm, ws, bs_b)


BF16_SUBLANES = 16
VT_ROWS = HEAD_DIM + BF16_SUBLANES

def _vt_kernel(v_ref, o_ref, *, tk):
    row = lax.broadcasted_iota(jnp.int32, (BF16_SUBLANES, tk), 0)
    ones_row = jnp.where(row == 0, 1.0, 0.0).astype(BF16)
    for h in range(B_HEADS):
        v = v_ref[:, h * HEAD_DIM:(h + 1) * HEAD_DIM].astype(F32)
        o_ref[h, :HEAD_DIM, :] = v.T.astype(BF16)
        o_ref[h, HEAD_DIM:, :] = ones_row


def _v_transposed(z3, *, tk):
    b, s, _ = z3.shape
    v_block = (2 * A_WIDTH + 2 * B_QK_WIDTH) // B_WIDTH
    return pl.pallas_call(
        functools.partial(_vt_kernel, tk=tk),
        grid=(b, s // tk),
        in_specs=[pl.BlockSpec((None, tk, B_WIDTH), lambda i, c: (i, c, v_block))],
        out_specs=pl.BlockSpec((None, B_HEADS, None, VT_ROWS, tk), lambda i, c: (i, 0, c, 0, 0)),
        out_shape=jax.ShapeDtypeStruct((b, B_HEADS, s // tk, VT_ROWS, tk), BF16),
        compiler_params=_params(("parallel", "parallel")),
        name="v_transpose",
    )(z3)


def _diff_attn_kernel(q_ref, k_ref, vt_ref, lam_ref, g_ref, o_ref,
                      qt_ref, s_ref, p_ref, a_ref, acc_ref, *, tq, tk, nck, lam_init):
    q = q_ref[...].astype(F32) * (B_QK_DIM ** -0.5 * LOG2E)
    qt = q.T
    row = lax.broadcasted_iota(jnp.int32, qt.shape, 0)
    qt_ref[:, :tq] = jnp.where(row < B_QK_DIM, qt, 0.0).astype(BF16)
    qt_ref[:, tq:] = jnp.where(row >= B_QK_DIM, qt, 0.0).astype(BF16)

    def scores(c):
        start = c * tk if isinstance(c, int) else pl.multiple_of(c * tk, tk)
        return jnp.dot(k_ref[pl.ds(start, tk), :], qt_ref[...], preferred_element_type=F32)

    def softmax(s, m_old):
        m_new = jnp.maximum(m_old, jnp.max(s, axis=0, keepdims=True))
        alpha = jnp.exp2(m_old - m_new)
        p = jnp.exp2(s - m_new).astype(BF16)
        return p, alpha, m_new

    def accumulate(c, p, alpha):
        acc_ref[...] = alpha * acc_ref[...] + jnp.dot(vt_ref[c], p, preferred_element_type=F32)

    def stage(c, slot, m, with_scores, with_values):
        if with_scores:
            s_ref[1 - slot] = scores(c + 1)
        p, alpha, m = softmax(s_ref[slot], m)
        if with_values:
            accumulate(c - 1, p_ref[1 - slot], a_ref[1 - slot])
        p_ref[slot] = p
        a_ref[slot] = alpha
        return m

    def two_stages(j, m):
        m = stage(2 * j + 1, 1, m, True, True)
        return stage(2 * j + 2, 0, m, True, True)

    s_ref[0] = scores(0)
    acc_ref[...] = jnp.zeros(acc_ref.shape, F32)
    m = jnp.full((1, 2 * tq), NEG_BIG, F32)
    m = stage(0, 0, m, True, False)
    m = lax.fori_loop(0, (nck - 2) // 2, two_stages, m)
    m = stage(nck - 1, 1, m, False, True)
    accumulate(nck - 1, p_ref[1], a_ref[1])

    lv = lam_ref[...]
    lam = (jnp.exp(jnp.sum(lv[0:1] * lv[1:2], axis=1, keepdims=True))
           - jnp.exp(jnp.sum(lv[2:3] * lv[3:4], axis=1, keepdims=True)) + lam_init)
    inv_l = 1.0 / acc_ref[HEAD_DIM:HEAD_DIM + 1, :]
    acc = acc_ref[:HEAD_DIM, :] * inv_l
    ot = acc[:, :tq] - lam * acc[:, tq:]
    o = ot.T
    o_ref[...] = (_rms(o, g_ref[...]) * (1.0 - lam_init)).astype(BF16)


def _diff_attn(z3, vt, lam_vecs, subnorm, lam_init, *, tq, tk):
    b, s, _ = z3.shape
    nck = s // tk
    assert s % tq == 0 and s % tk == 0 and nck % 2 == 0
    q_block = 2 * A_WIDTH // LANES
    k_block = (2 * A_WIDTH + B_QK_WIDTH) // LANES
    kern = functools.partial(_diff_attn_kernel, tq=tq, tk=tk, nck=nck, lam_init=lam_init)
    return pl.pallas_call(
        kern,
        grid=(b, B_HEADS, s // tq),
        in_specs=[
            pl.BlockSpec((None, tq, LANES), lambda i, h, t: (i, t, q_block + h)),
            pl.BlockSpec((None, s, LANES), lambda i, h, t: (i, 0, k_block + h)),
            pl.BlockSpec((None, None, nck, VT_ROWS, tk), lambda i, h, t: (i, h, 0, 0, 0)),
            pl.BlockSpec((4, B_QK_DIM), lambda i, h, t: (0, 0)),
            pl.BlockSpec((1, HEAD_DIM), lambda i, h, t: (0, 0)),
        ],
        out_specs=pl.BlockSpec((None, tq, HEAD_DIM), lambda i, h, t: (i, t, h)),
        out_shape=jax.ShapeDtypeStruct((b, s, B_WIDTH), BF16),
        scratch_shapes=[
            pltpu.VMEM((HEAD_DIM, 2 * tq), BF16),
            pltpu.VMEM((2, tk, 2 * tq), F32),
            pltpu.VMEM((2, tk, 2 * tq), BF16),
            pltpu.VMEM((2, 1, 2 * tq), F32),
            pltpu.VMEM((VT_ROWS, 2 * tq), F32),
        ],
        compiler_params=_params(("parallel", "parallel", "arbitrary")),
        name="diff_attn",
    )(z3, z3, vt, lam_vecs, subnorm)


def _win_attn_kernel(sink_ref, q_ref, kp_ref, kc_ref, kn_ref, vp_ref, vc_ref, vn_ref, o_ref, *, nb):
    n = pl.program_id(1)
    rows = C_GROUP * CHUNK
    r = lax.broadcasted_iota(jnp.int32, (rows, 3 * CHUNK), 0) % CHUNK
    c = lax.broadcasted_iota(jnp.int32, (rows, 3 * CHUNK), 1)
    rel = c - CHUNK - r
    valid = jnp.logical_and(rel <= C_WINDOW, rel >= -C_WINDOW)
    valid = jnp.logical_and(valid, jnp.logical_or(n > 0, c >= CHUNK))
    valid = jnp.logical_and(valid, jnp.logical_or(n < nb - 1, c < 2 * CHUNK))
    lane = lax.broadcasted_iota(jnp.int32, (3 * CHUNK, LANES), 1)
    ones_col = jnp.where(lane == 0, 1.0, 0.0).astype(BF16)
    scale = HEAD_DIM ** -0.5 * LOG2E
    for kh in range(C_KV_HEADS):
        kcols = slice(kh * HEAD_DIM, (kh + 1) * HEAD_DIM)
        kband = jnp.concatenate([kp_ref[:, kcols], kc_ref[:, kcols], kn_ref[:, kcols]], axis=0)
        vband = jnp.concatenate([vp_ref[:, kcols], vc_ref[:, kcols], vn_ref[:, kcols]], axis=0)
        v_aug = jnp.concatenate([vband, ones_col], axis=1)
        heads = [kh * C_GROUP + g for g in range(C_GROUP)]
        q = jnp.concatenate([q_ref[:, h * HEAD_DIM:(h + 1) * HEAD_DIM] for h in heads], axis=0)
        q = (q.astype(F32) * scale).astype(BF16)
        sink = jnp.concatenate([jnp.full((CHUNK, 1), sink_ref[h] * LOG2E, F32) for h in heads], axis=0)
        s = lax.dot_general(q, kband, (((1,), (1,)), ((), ())), preferred_element_type=F32)
        s = jnp.where(valid, s, NEG_BIG)
        m = jnp.maximum(jnp.max(s, axis=-1, keepdims=True), sink)
        p = jnp.exp2((s - m).astype(BF16))
        o_aug = jnp.dot(p, v_aug, preferred_element_type=F32)
        denom = o_aug[:, HEAD_DIM:HEAD_DIM + 1] + jnp.exp2(sink - m)
        o = o_aug[:, :HEAD_DIM] / denom
        for g, h in enumerate(heads):
            o_ref[:, h * HEAD_DIM:(h + 1) * HEAD_DIM] = o[g * CHUNK:(g + 1) * CHUNK, :].astype(BF16)


def _win_attn(z3, sink):
    b, s, _ = z3.shape
    nb = s // CHUNK
    kv_w = C_KV_HEADS * HEAD_DIM
    k_block = C_HEADS * HEAD_DIM // kv_w
    v_block = k_block + 1
    prev = lambda n: jnp.maximum(n - 1, 0)
    nxt = lambda n: jnp.minimum(n + 1, nb - 1)

    def kv_spec(blk, f):
        return pl.BlockSpec((None, CHUNK, kv_w), lambda i, n, sk: (i, f(n), blk))

    same = lambda n: n
    grid_spec = pltpu.PrefetchScalarGridSpec(
        num_scalar_prefetch=1,
        grid=(b, nb),
        in_specs=[
            pl.BlockSpec((None, CHUNK, C_HEADS * HEAD_DIM), lambda i, n, sk: (i, n, 0)),
            kv_spec(k_block, prev), kv_spec(k_block, same), kv_spec(k_block, nxt),
            kv_spec(v_block, prev), kv_spec(v_block, same), kv_spec(v_block, nxt),
        ],
        out_specs=pl.BlockSpec((None, CHUNK, C_HEADS * HEAD_DIM), lambda i, n, sk: (i, n, 0)),
    )
    return pl.pallas_call(
        functools.partial(_win_attn_kernel, nb=nb),
        grid_spec=grid_spec,
        out_shape=jax.ShapeDtypeStruct((b, s, C_HEADS * HEAD_DIM), BF16),
        compiler_params=_params(("parallel", "parallel")),
        name="win_attn",
    )(sink, z3, z3, z3, z3, z3, z3, z3)


def _out_proj_kernel(*refs, n_in):
    a_refs = refs[:n_in]
    w_refs = refs[n_in:2 * n_in]
    x_ref, g_ref, o_ref = refs[2 * n_in:]
    acc = jnp.dot(a_refs[0][...], w_refs[0][...], preferred_element_type=F32)
    for a_ref, w_ref in zip(a_refs[1:], w_refs[1:]):
        acc = acc + jnp.dot(a_ref[...], w_ref[...], preferred_element_type=F32)
    o_ref[...] = x_ref[...] + _rms(acc, g_ref[...])


def _out_proj(acts, ws, x, g, *, bm=512):
    m, d = x.shape
    bm = min(bm, m)
    n_in = len(acts)
    in_specs = [pl.BlockSpec((bm, a.shape[1]), lambda i: (i, 0)) for a in acts]
    in_specs += [pl.BlockSpec(w.shape, lambda i: (0, 0)) for w in ws]
    in_specs += [pl.BlockSpec((bm, d), lambda i: (i, 0)), pl.BlockSpec((1, d), lambda i: (0, 0))]
    return pl.pallas_call(
        functools.partial(_out_proj_kernel, n_in=n_in),
        grid=(m // bm,),
        in_specs=in_specs,
        out_specs=pl.BlockSpec((bm, d), lambda i: (i, 0)),
        out_shape=jax.ShapeDtypeStruct((m, d), F32),
        compiler_params=_params(("parallel",)),
        name="out_proj",
    )(*acts, *ws, x, g)


def _mlp_kernel(x_ref, g1_ref, wu_ref, wd_ref, g2_ref, o_ref, h_ref, acc_ref):
    f = pl.program_id(1)

    @pl.when(f == 0)
    def _():
        h_ref[...] = _rms(x_ref[...], g1_ref[...]).astype(BF16)
        acc_ref[...] = jnp.zeros(acc_ref.shape, F32)

    a = jnp.dot(h_ref[...], wu_ref[...], preferred_element_type=F32)
    a = jnp.square(jnp.maximum(a, 0.0)).astype(BF16)
    acc_ref[...] += jnp.dot(a, wd_ref[...], preferred_element_type=F32)

    @pl.when(f == pl.num_programs(1) - 1)
    def _():
        o_ref[...] = x_ref[...] + _rms(acc_ref[...], g2_ref[...])


def _mlp(x, g1, wu, wd, g2, *, bm=512, tf=1024):
    m, d = x.shape
    dff = wu.shape[1]
    bm = min(bm, m)
    return pl.pallas_call(
        _mlp_kernel,
        grid=(m // bm, dff // tf),
        in_specs=[
            pl.BlockSpec((bm, d), lambda i, f: (i, 0)),
            pl.BlockSpec((1, d), lambda i, f: (0, 0)),
            pl.BlockSpec((d, tf), lambda i, f: (0, f)),
            pl.BlockSpec((tf, d), lambda i, f: (f, 0)),
            pl.BlockSpec((1, d), lambda i, f: (0, 0)),
        ],
        out_specs=pl.BlockSpec((bm, d), lambda i, f: (i, 0)),
        out_shape=jax.ShapeDtypeStruct((m, d), F32),
        scratch_shapes=[pltpu.VMEM((bm, d), BF16), pltpu.VMEM((bm, d), F32)],
        compiler_params=_params(("parallel", "arbitrary")),
        name="mlp",
    )(x, g1, wu, wd, g2)


def _pair_heads(w):
    d = w.shape[0]
    return w.reshape(d, 2, B_HEADS, B_QK_DIM).transpose(0, 2, 1, 3).reshape(d, B_QK_WIDTH)


def _prep_weights(p):
    row = lambda a: a.reshape(a.shape[0], 1, a.shape[-1]).astype(F32)
    w_in_ab = p["w_in_ab"]
    q0, k0, v0 = 2 * A_WIDTH, 2 * A_WIDTH + B_QK_WIDTH, 2 * A_WIDTH + 2 * B_QK_WIDTH
    w_in_ab = jnp.concatenate([
        w_in_ab[:, :, :q0],
        jax.vmap(_pair_heads)(w_in_ab[:, :, q0:k0]),
        jax.vmap(_pair_heads)(w_in_ab[:, :, k0:v0]),
        w_in_ab[:, :, v0:],
    ], axis=-1).astype(BF16)
    return dict(
        norm_mix_pre=row(p["norm_mix_pre"]), norm_mix_post=row(p["norm_mix_post"]),
        norm_mlp_pre=row(p["norm_mlp_pre"]), norm_mlp_post=row(p["norm_mlp_post"]),
        w_in_ab=w_in_ab,
        w_out_ab=p["w_out_ab"].astype(BF16),
        a_vnorm=p["a_vnorm"].astype(F32),
        a_ws=p["a_ws"].astype(BF16),
        a_bs_b=jnp.broadcast_to(p["a_bs"].astype(F32)[..., None],
                                p["a_bs"].shape + (HEAD_DIM,)),
        b_lambda=p["b_lambda"].astype(F32),
        b_subnorm=p["b_subnorm"].astype(F32)[:, None, :],
        w_in_c=p["w_in_c"].astype(BF16),
        w_out_c=p["w_out_c"].astype(BF16),
        c_sink=p["c_sink"].astype(F32),
        w_up=p["w_up"].astype(BF16),
        w_down=p["w_down"].astype(BF16),
    )


def _trunk(x3, p):
    b, s, d = x3.shape
    m = b * s
    x = x3.reshape(m, d)
    tq = min(2048, s)
    tk = min(512, s // 2)
    tab_ab = _rope_tables(s, B_QK_DIM, B_QK_DIM // ROPE_FRACTION)
    tab_c = _rope_tables(s, HEAD_DIM, HEAD_DIM // ROPE_FRACTION)
    for i in range(DEPTH):
        j = i // 2
        if i % 2 == 0:
            lam_init = 0.8 - 0.6 * math.exp(-0.3 * i)
            z = _in_proj(x, p["norm_mix_pre"][i], p["w_in_ab"][j], tab_ab, s, bn=1024,
                         rope_cols=(2 * A_WIDTH, 2 * A_WIDTH + 2 * B_QK_WIDTH),
                         half=B_QK_DIM // ROPE_FRACTION // 2)
            a_out = _gmlp(z, p["a_vnorm"][j], p["a_ws"][j], p["a_bs_b"][j])
            z3 = z.reshape(b, s, AB_IN)
            vt = _v_transposed(z3, tk=tk)
            b_out = _diff_attn(z3, vt, p["b_lambda"][j], p["b_subnorm"][j], lam_init, tq=tq, tk=tk)
            w_out = p["w_out_ab"][j]
            x = _out_proj([a_out, b_out.reshape(m, B_WIDTH)], [w_out[:A_WIDTH], w_out[A_WIDTH:]],
                          x, p["norm_mix_post"][i])
        else:
            z = _in_proj(x, p["norm_mix_pre"][i], p["w_in_c"][j], tab_c, s, bn=1024,
                         rope_cols=(0, (C_HEADS + C_KV_HEADS) * HEAD_DIM),
                         half=HEAD_DIM // ROPE_FRACTION // 2)
            c_out = _win_attn(z.reshape(b, s, C_IN), p["c_sink"][j])
            x = _out_proj([c_out.reshape(m, D_MODEL)], [p["w_out_c"][j]], x, p["norm_mix_post"][i])
        x = _mlp(x, p["norm_mlp_pre"][i], p["w_up"][i], p["w_down"][i], p["norm_mlp_post"][i])
    return x.reshape(b, s, d)


def kernel(x_prompt, x_sample, norm_mix_pre, norm_mix_post, norm_mlp_pre, norm_mlp_post,
           w_in_ab, w_out_ab, a_vnorm, a_ws, a_bs, b_lambda, b_subnorm,
           w_in_c, w_out_c, c_sink, w_up, w_down):
    p = _prep_weights(dict(
        norm_mix_pre=norm_mix_pre, norm_mix_post=norm_mix_post,
        norm_mlp_pre=norm_mlp_pre, norm_mlp_post=norm_mlp_post,
        w_in_ab=w_in_ab, w_out_ab=w_out_ab, a_vnorm=a_vnorm, a_ws=a_ws, a_bs=a_bs,
        b_lambda=b_lambda, b_subnorm=b_subnorm, w_in_c=w_in_c, w_out_c=w_out_c,
        c_sink=c_sink, w_up=w_up, w_down=w_down))
    return (_trunk(x_prompt, p), _trunk(x_sample, p))
```

```python
import functools
import math

import jax
import jax.numpy as jnp
from jax import lax
from jax.experimental import pallas as pl
from jax.experimental.pallas import tpu as pltpu

F32 = jnp.float32
BF16 = jnp.bfloat16

D_MODEL = 2048
DEPTH = 4
HEAD_DIM = 128
CHUNK = 128
A_GROUPS = 8
A_WIDTH = 1024
B_HEADS = 8
B_QK_DIM = 64
B_QK_WIDTH = 1024
B_WIDTH = 1024
AB_IN = 5120
C_HEADS = 16
C_KV_HEADS = 4
C_GROUP = 4
C_WINDOW = 128
C_IN = 3072
D_FF = 8192
ROPE_THETA = 500000.0
ROPE_FRACTION = 4
EPS = 1e-6
LOG2E = 1.4426950408889634
NEG_BIG = -1e30

LANES = 128
VMEM_LIMIT = 56 * 1024 * 1024


def _params(sem, limit=VMEM_LIMIT, flags=None):
    return pltpu.CompilerParams(dimension_semantics=sem, vmem_limit_bytes=limit, flags=flags)


def _rms(x, g):
    return x * lax.rsqrt(jnp.mean(x * x, axis=-1, keepdims=True) + EPS) * g


def _in_proj_kernel(x_ref, g_ref, w_ref, cos_ref, sa_ref, sb_ref, o_ref, h_ref, *,
                    rope_lo, rope_hi, half, bn):
    j = pl.program_id(1)
    chunks = bn // LANES

    @pl.when(j == 0)
    def _():
        h_ref[...] = _rms(x_ref[...], g_ref[...]).astype(BF16)

    def project(n_rope):
        z = jnp.dot(h_ref[...], w_ref[...], preferred_element_type=F32)
        if n_rope:
            cos = cos_ref[...]
            sa = sa_ref[...]
            sb = sb_ref[...]
        for c in range(n_rope):
            zc = z[:, c * LANES:(c + 1) * LANES]
            zr = (zc * cos + pltpu.roll(zc, half, 1) * sa
                  + pltpu.roll(zc, LANES - half, 1) * sb)
            o_ref[:, c * LANES:(c + 1) * LANES] = zr.astype(BF16)
        if n_rope < chunks:
            o_ref[:, n_rope * LANES:] = z[:, n_rope * LANES:].astype(BF16)

    first_full, end_full, tail = rope_lo // chunks, rope_hi // chunks, rope_hi % chunks
    pl.when(jnp.logical_and(j >= first_full, j < end_full))(lambda: project(chunks))
    if tail:
        pl.when(j == end_full)(lambda: project(tail))
    n_partial = end_full + (1 if tail else 0)
    pl.when(jnp.logical_or(j < first_full, j >= n_partial))(lambda: project(0))


def _in_proj(x, g, w, tables, seq, *, bn, rope_cols, half, bm=1024):
    m, d = x.shape
    n = w.shape[1]
    bm = min(bm, seq)
    nsb = seq // bm
    cos, sa, sb = tables
    assert rope_cols[0] % bn == 0 and rope_cols[1] % LANES == 0 and seq % bm == 0
    kern = functools.partial(_in_proj_kernel, rope_lo=rope_cols[0] // LANES,
                             rope_hi=rope_cols[1] // LANES, half=half, bn=bn)
    tab_spec = pl.BlockSpec((bm, LANES), lambda i, j: (i % nsb, 0))
    return pl.pallas_call(
        kern,
        grid=(m // bm, n // bn),
        in_specs=[
            pl.BlockSpec((bm, d), lambda i, j: (i, 0)),
            pl.BlockSpec((1, d), lambda i, j: (0, 0)),
            pl.BlockSpec((d, bn), lambda i, j: (0, j)),
            tab_spec, tab_spec, tab_spec,
        ],
        out_specs=pl.BlockSpec((bm, bn), lambda i, j: (i, j)),
        out_shape=jax.ShapeDtypeStruct((m, n), BF16),
        scratch_shapes=[pltpu.VMEM((bm, d), BF16)],
        compiler_params=_params(("parallel", "arbitrary")),
        name="in_proj",
    )(x, g, w, cos, sa, sb)


def _rope_tables(seq, head_dim, rot):
    half = rot // 2
    inv = ROPE_THETA ** (-jnp.arange(half, dtype=F32) / half)
    ang = jnp.arange(seq, dtype=F32)[:, None] * inv[None, :]
    c, s = jnp.cos(ang), jnp.sin(ang)
    z_half = jnp.zeros((seq, half), F32)
    z_rest = jnp.zeros((seq, head_dim - rot), F32)
    cos = jnp.concatenate([c, c, jnp.ones((seq, head_dim - rot), F32)], axis=1)
    sa = jnp.concatenate([z_half, s, z_rest], axis=1)
    sb = jnp.concatenate([-s, z_half, z_rest], axis=1)
    reps = LANES // head_dim
    return tuple(jnp.tile(t, (1, reps)) for t in (cos, sa, sb))


def _gmlp_kernel(u_ref, v_ref, vn_ref, ws_ref, bs_ref, o_ref, *, tm):
    for g in range(A_GROUPS):
        cols = slice(g * HEAD_DIM, (g + 1) * HEAD_DIM)
        u = jax.nn.gelu(u_ref[:, cols].astype(F32))
        v = jax.nn.gelu(v_ref[:, cols].astype(F32))
        v = _rms(v, vn_ref[g:g + 1, :]).astype(BF16)
        w = ws_ref[g]
        b = bs_ref[g]
        for c in range(tm // CHUNK):
            rows = slice(c * CHUNK, (c + 1) * CHUNK)
            mixed = jnp.dot(w, v[rows, :], preferred_element_type=F32) + b
            o_ref[rows, cols] = (u[rows, :] * mixed).astype(BF16)


def _gmlp(z, vnorm, ws, bs_b, *, tm=512):
    m = z.shape[0]
    tm = min(tm, m)
    return pl.pallas_call(
        functools.partial(_gmlp_kernel, tm=tm),
        grid=(m // tm,),
        in_specs=[
            pl.BlockSpec((tm, A_WIDTH), lambda i: (i, 0)),
            pl.BlockSpec((tm, A_WIDTH), lambda i: (i, 1)),
            pl.BlockSpec((A_GROUPS, HEAD_DIM), lambda i: (0, 0)),
            pl.BlockSpec((A_GROUPS, CHUNK, CHUNK), lambda i: (0, 0, 0)),
            pl.BlockSpec((A_GROUPS, CHUNK, HEAD_DIM), lambda i: (0, 0, 0)),
        ],
        out_specs=pl.BlockSpec((tm, A_WIDTH), lambda i: (i, 0)),
        out_shape=jax.ShapeDtypeStruct((m, A_WIDTH), BF16),
        compiler_params=_params(("parallel",)),
        name="gmlp",
    )(z, z, vnorm, ws, bs_b)


BF16_SUBLANES = 16
VT_ROWS = HEAD_DIM + BF16_SUBLANES

def _vt_kernel(v_ref, o_ref, *, tk):
    row = lax.broadcasted_iota(jnp.int32, (BF16_SUBLANES, tk), 0)
    ones_row = jnp.where(row == 0, 1.0, 0.0).astype(BF16)
    for h in range(B_HEADS):
        v = v_ref[:, h * HEAD_DIM:(h + 1) * HEAD_DIM].astype(F32)
        o_ref[h, :HEAD_DIM, :] = v.T.astype(BF16)
        o_ref[h, HEAD_DIM:, :] = ones_row


def _v_transposed(z3, *, tk):
    b, s, _ = z3.shape
    v_block = (2 * A_WIDTH + 2 * B_QK_WIDTH) // B_WIDTH
    return pl.pallas_call(
        functools.partial(_vt_kernel, tk=tk),
        grid=(b, s // tk),
        in_specs=[pl.BlockSpec((None, tk, B_WIDTH), lambda i, c: (i, c, v_block))],
        out_specs=pl.BlockSpec((None, B_HEADS, None, VT_ROWS, tk), lambda i, c: (i, 0, c, 0, 0)),
        out_shape=jax.ShapeDtypeStruct((b, B_HEADS, s // tk, VT_ROWS, tk), BF16),
        compiler_params=_params(("parallel", "parallel")),
        name="v_transpose",
    )(z3)


def _diff_attn_kernel(q_ref, k_ref, vt_ref, lam_ref, g_ref, o_ref,
                      qt_ref, s_ref, p_ref, a_ref, acc_ref, *, tq, tk, nck, lam_init):
    q = q_ref[...].astype(F32) * (B_QK_DIM ** -0.5 * LOG2E)
    qt = q.T
    row = lax.broadcasted_iota(jnp.int32, qt.shape, 0)
    qt_ref[:, :tq] = jnp.where(row < B_QK_DIM, qt, 0.0).astype(BF16)
    qt_ref[:, tq:] = jnp.where(row >= B_QK_DIM, qt, 0.0).astype(BF16)

    def scores(c):
        start = c * tk if isinstance(c, int) else pl.multiple_of(c * tk, tk)
        return jnp.dot(k_ref[pl.ds(start, tk), :], qt_ref[...], preferred_element_type=F32)

    def softmax(s, m_old):
        m_new = jnp.maximum(m_old, jnp.max(s, axis=0, keepdims=True))
        alpha = jnp.exp2(m_old - m_new)
        p = jnp.exp2(s - m_new).astype(BF16)
        return p, alpha, m_new

    def accumulate(c, p, alpha):
        acc_ref[...] = alpha * acc_ref[...] + jnp.dot(vt_ref[c], p, preferred_element_type=F32)

    def stage(c, slot, m, with_scores, with_values):
        if with_scores:
            s_ref[1 - slot] = scores(c + 1)
        p, alpha, m = softmax(s_ref[slot], m)
        if with_values:
            accumulate(c - 1, p_ref[1 - slot], a_ref[1 - slot])
        p_ref[slot] = p
        a_ref[slot] = alpha
        return m

    def two_stages(j, m):
        m = stage(2 * j + 1, 1, m, True, True)
        return stage(2 * j + 2, 0, m, True, True)

    s_ref[0] = scores(0)
    acc_ref[...] = jnp.zeros(acc_ref.shape, F32)
    m = jnp.full((1, 2 * tq), NEG_BIG, F32)
    m = stage(0, 0, m, True, False)
    m = lax.fori_loop(0, (nck - 2) // 2, two_stages, m)
    m = stage(nck - 1, 1, m, False, True)
    accumulate(nck - 1, p_ref[1], a_ref[1])

    lv = lam_ref[...]
    lam = (jnp.exp(jnp.sum(lv[0:1] * lv[1:2], axis=1, keepdims=True))
           - jnp.exp(jnp.sum(lv[2:3] * lv[3:4], axis=1, keepdims=True)) + lam_init)
    inv_l = 1.0 / acc_ref[HEAD_DIM:HEAD_DIM + 1, :]
    acc = acc_ref[:HEAD_DIM, :] * inv_l
    ot = acc[:, :tq] - lam * acc[:, tq:]
    o = ot.T
    o_ref[...] = (_rms(o, g_ref[...]) * (1.0 - lam_init)).astype(BF16)


def _diff_attn(z3, vt, lam_vecs, subnorm, lam_init, *, tq, tk):
    b, s, _ = z3.shape
    nck = s // tk
    assert s % tq == 0 and s % tk == 0 and nck % 2 == 0
    q_block = 2 * A_WIDTH // LANES
    k_block = (2 * A_WIDTH + B_QK_WIDTH) // LANES
    kern = functools.partial(_diff_attn_kernel, tq=tq, tk=tk, nck=nck, lam_init=lam_init)
    return pl.pallas_call(
        kern,
        grid=(b, B_HEADS, s // tq),
        in_specs=[
            pl.BlockSpec((None, tq, LANES), lambda i, h, t: (i, t, q_block + h)),
            pl.BlockSpec((None, s, LANES), lambda i, h, t: (i, 0, k_block + h)),
            pl.BlockSpec((None, None, nck, VT_ROWS, tk), lambda i, h, t: (i, h, 0, 0, 0)),
            pl.BlockSpec((4, B_QK_DIM), lambda i, h, t: (0, 0)),
            pl.BlockSpec((1, HEAD_DIM), lambda i, h, t: (0, 0)),
        ],
        out_specs=pl.BlockSpec((None, tq, HEAD_DIM), lambda i, h, t: (i, t, h)),
        out_shape=jax.ShapeDtypeStruct((b, s, B_WIDTH), BF16),
        scratch_shapes=[
            pltpu.VMEM((HEAD_DIM, 2 * tq), BF16),
            pltpu.VMEM((2, tk, 2 * tq), F32),
            pltpu.VMEM((2, tk, 2 * tq), BF16),
            pltpu.VMEM((2, 1, 2 * tq), F32),
            pltpu.VMEM((VT_ROWS, 2 * tq), F32),
        ],
        compiler_params=_params(("parallel", "parallel", "arbitrary")),
        name="diff_attn",
    )(z3, z3, vt, lam_vecs, subnorm)


def _win_attn_kernel(sink_ref, q_ref, kp_ref, kc_ref, kn_ref, vp_ref, vc_ref, vn_ref, o_ref, *, nb):
    n = pl.program_id(1)
    rows = C_GROUP * CHUNK
    r = lax.broadcasted_iota(jnp.int32, (rows, 3 * CHUNK), 0) % CHUNK
    c = lax.broadcasted_iota(jnp.int32, (rows, 3 * CHUNK), 1)
    rel = c - CHUNK - r
    valid = jnp.logical_and(rel <= C_WINDOW, rel >= -C_WINDOW)
    valid = jnp.logical_and(valid, jnp.logical_or(n > 0, c >= CHUNK))
    valid = jnp.logical_and(valid, jnp.logical_or(n < nb - 1, c < 2 * CHUNK))
    ones_block = jnp.ones((3 * CHUNK, LANES), BF16)
    scale = HEAD_DIM ** -0.5 * LOG2E
    for kh in range(C_KV_HEADS):
        kcols = slice(kh * HEAD_DIM, (kh + 1) * HEAD_DIM)
        kband = jnp.concatenate([kp_ref[:, kcols], kc_ref[:, kcols], kn_ref[:, kcols]], axis=0)
        vband = jnp.concatenate([vp_ref[:, kcols], vc_ref[:, kcols], vn_ref[:, kcols]], axis=0)
        v_aug = jnp.concatenate([vband, ones_block], axis=1)
        heads = [kh * C_GROUP + g for g in range(C_GROUP)]
        q = jnp.concatenate([q_ref[:, h * HEAD_DIM:(h + 1) * HEAD_DIM] for h in heads], axis=0)
        q = (q.astype(F32) * scale).astype(BF16)
        sink = jnp.concatenate([jnp.full((CHUNK, LANES), sink_ref[h] * LOG2E, F32) for h in heads], axis=0)
        s = lax.dot_general(q, kband, (((1,), (1,)), ((), ())), preferred_element_type=F32)
        s = jnp.where(valid, s, NEG_BIG)
        m = jnp.maximum(jnp.broadcast_to(jnp.max(s, axis=-1, keepdims=True), sink.shape), sink)
        p = jnp.exp2(s - jnp.concatenate([m, m, m], axis=1)).astype(BF16)
        o_aug = jnp.dot(p, v_aug, preferred_element_type=F32)
        denom = o_aug[:, HEAD_DIM:] + jnp.exp2(sink - m)
        o = o_aug[:, :HEAD_DIM] / denom
        for g, h in enumerate(heads):
            o_ref[:, h * HEAD_DIM:(h + 1) * HEAD_DIM] = o[g * CHUNK:(g + 1) * CHUNK, :].astype(BF16)


def _win_attn(z3, sink):
    b, s, _ = z3.shape
    nb = s // CHUNK
    kv_w = C_KV_HEADS * HEAD_DIM
    k_block = C_HEADS * HEAD_DIM // kv_w
    v_block = k_block + 1
    prev = lambda n: jnp.maximum(n - 1, 0)
    nxt = lambda n: jnp.minimum(n + 1, nb - 1)

    def kv_spec(blk, f):
        return pl.BlockSpec((None, CHUNK, kv_w), lambda i, n, sk: (i, f(n), blk))

    same = lambda n: n
    grid_spec = pltpu.PrefetchScalarGridSpec(
        num_scalar_prefetch=1,
        grid=(b, nb),
        in_specs=[
            pl.BlockSpec((None, CHUNK, C_HEADS * HEAD_DIM), lambda i, n, sk: (i, n, 0)),
            kv_spec(k_block, prev), kv_spec(k_block, same), kv_spec(k_block, nxt),
            kv_spec(v_block, prev), kv_spec(v_block, same), kv_spec(v_block, nxt),
        ],
        out_specs=pl.BlockSpec((None, CHUNK, C_HEADS * HEAD_DIM), lambda i, n, sk: (i, n, 0)),
    )
    return pl.pallas_call(
        functools.partial(_win_attn_kernel, nb=nb),
        grid_spec=grid_spec,
        out_shape=jax.ShapeDtypeStruct((b, s, C_HEADS * HEAD_DIM), BF16),
        compiler_params=_params(("parallel", "parallel")),
        name="win_attn",
    )(sink, z3, z3, z3, z3, z3, z3, z3)


def _out_proj_kernel(*refs, n_in):
    a_refs = refs[:n_in]
    w_refs = refs[n_in:2 * n_in]
    x_ref, g_ref, o_ref = refs[2 * n_in:]
    acc = jnp.dot(a_refs[0][...], w_refs[0][...], preferred_element_type=F32)
    for a_ref, w_ref in zip(a_refs[1:], w_refs[1:]):
        acc = acc + jnp.dot(a_ref[...], w_ref[...], preferred_element_type=F32)
    o_ref[...] = x_ref[...] + _rms(acc, g_ref[...])


def _out_proj(acts, ws, x, g, *, bm=512):
    m, d = x.shape
    bm = min(bm, m)
    n_in = len(acts)
    in_specs = [pl.BlockSpec((bm, a.shape[1]), lambda i: (i, 0)) for a in acts]
    in_specs += [pl.BlockSpec(w.shape, lambda i: (0, 0)) for w in ws]
    in_specs += [pl.BlockSpec((bm, d), lambda i: (i, 0)), pl.BlockSpec((1, d), lambda i: (0, 0))]
    return pl.pallas_call(
        functools.partial(_out_proj_kernel, n_in=n_in),
        grid=(m // bm,),
        in_specs=in_specs,
        out_specs=pl.BlockSpec((bm, d), lambda i: (i, 0)),
        out_shape=jax.ShapeDtypeStruct((m, d), F32),
        compiler_params=_params(("parallel",)),
        name="out_proj",
    )(*acts, *ws, x, g)


def _mlp_kernel(x_ref, g1_ref, wu_ref, wd_ref, g2_ref, o_ref, h_ref, acc_ref):
    f = pl.program_id(1)

    @pl.when(f == 0)
    def _():
        h_ref[...] = _rms(x_ref[...], g1_ref[...]).astype(BF16)
        acc_ref[...] = jnp.zeros(acc_ref.shape, F32)

    a = jnp.dot(h_ref[...], wu_ref[...], preferred_element_type=F32)
    a = jnp.square(jnp.maximum(a, 0.0)).astype(BF16)
    acc_ref[...] += jnp.dot(a, wd_ref[...], preferred_element_type=F32)

    @pl.when(f == pl.num_programs(1) - 1)
    def _():
        o_ref[...] = x_ref[...] + _rms(acc_ref[...], g2_ref[...])


def _mlp(x, g1, wu, wd, g2, *, bm=512, tf=1024):
    m, d = x.shape
    dff = wu.shape[1]
    bm = min(bm, m)
    return pl.pallas_call(
        _mlp_kernel,
        grid=(m // bm, dff // tf),
        in_specs=[
            pl.BlockSpec((bm, d), lambda i, f: (i, 0)),
            pl.BlockSpec((1, d), lambda i, f: (0, 0)),
            pl.BlockSpec((d, tf), lambda i, f: (0, f)),
            pl.BlockSpec((tf, d), lambda i, f: (f, 0)),
            pl.BlockSpec((1, d), lambda i, f: (0, 0)),
        ],
        out_specs=pl.BlockSpec((bm, d), lambda i, f: (i, 0)),
        out_shape=jax.ShapeDtypeStruct((m, d), F32),
        scratch_shapes=[pltpu.VMEM((bm, d), BF16), pltpu.VMEM((bm, d), F32)],
        compiler_params=_params(("parallel", "arbitrary")),
        name="mlp",
    )(x, g1, wu, wd, g2)


def _pair_heads(w):
    d = w.shape[0]
    return w.reshape(d, 2, B_HEADS, B_QK_DIM).transpose(0, 2, 1, 3).reshape(d, B_QK_WIDTH)


def _prep_weights(p):
    row = lambda a: a.reshape(a.shape[0], 1, a.shape[-1]).astype(F32)
    w_in_ab = p["w_in_ab"]
    q0, k0, v0 = 2 * A_WIDTH, 2 * A_WIDTH + B_QK_WIDTH, 2 * A_WIDTH + 2 * B_QK_WIDTH
    w_in_ab = jnp.concatenate([
        w_in_ab[:, :, :q0],
        jax.vmap(_pair_heads)(w_in_ab[:, :, q0:k0]),
        jax.vmap(_pair_heads)(w_in_ab[:, :, k0:v0]),
        w_in_ab[:, :, v0:],
    ], axis=-1).astype(BF16)
    return dict(
        norm_mix_pre=row(p["norm_mix_pre"]), norm_mix_post=row(p["norm_mix_post"]),
        norm_mlp_pre=row(p["norm_mlp_pre"]), norm_mlp_post=row(p["norm_mlp_post"]),
        w_in_ab=w_in_ab,
        w_out_ab=p["w_out_ab"].astype(BF16),
        a_vnorm=p["a_vnorm"].astype(F32),
        a_ws=p["a_ws"].astype(BF16),
        a_bs_b=jnp.broadcast_to(p["a_bs"].astype(F32)[..., None],
                                p["a_bs"].shape + (HEAD_DIM,)),
        b_lambda=p["b_lambda"].astype(F32),
        b_subnorm=p["b_subnorm"].astype(F32)[:, None, :],
        w_in_c=p["w_in_c"].astype(BF16),
        w_out_c=p["w_out_c"].astype(BF16),
        c_sink=p["c_sink"].astype(F32),
        w_up=p["w_up"].astype(BF16),
        w_down=p["w_down"].astype(BF16),
    )


def _trunk(x3, p):
    b, s, d = x3.shape
    m = b * s
    x = x3.reshape(m, d)
    tq = min(2048, s)
    tk = min(512, s // 2)
    tab_ab = _rope_tables(s, B_QK_DIM, B_QK_DIM // ROPE_FRACTION)
    tab_c = _rope_tables(s, HEAD_DIM, HEAD_DIM // ROPE_FRACTION)
    for i in range(DEPTH):
        j = i // 2
        if i % 2 == 0:
            lam_init = 0.8 - 0.6 * math.exp(-0.3 * i)
            z = _in_proj(x, p["norm_mix_pre"][i], p["w_in_ab"][j], tab_ab, s, bn=1024,
                         rope_cols=(2 * A_WIDTH, 2 * A_WIDTH + 2 * B_QK_WIDTH),
                         half=B_QK_DIM // ROPE_FRACTION // 2)
            a_out = _gmlp(z, p["a_vnorm"][j], p["a_ws"][j], p["a_bs_b"][j])
            z3 = z.reshape(b, s, AB_IN)
            vt = _v_transposed(z3, tk=tk)
            b_out = _diff_attn(z3, vt, p["b_lambda"][j], p["b_subnorm"][j], lam_init, tq=tq, tk=tk)
            w_out = p["w_out_ab"][j]
            x = _out_proj([a_out, b_out.reshape(m, B_WIDTH)], [w_out[:A_WIDTH], w_out[A_WIDTH:]],
                          x, p["norm_mix_post"][i])
        else:
            z = _in_proj(x, p["norm_mix_pre"][i], p["w_in_c"][j], tab_c, s, bn=1024,
                         rope_cols=(0, (C_HEADS + C_KV_HEADS) * HEAD_DIM),
                         half=HEAD_DIM // ROPE_FRACTION // 2)
            c_out = _win_attn(z.reshape(b, s, C_IN), p["c_sink"][j])
            x = _out_proj([c_out.reshape(m, D_MODEL)], [p["w_out_c"][j]], x, p["norm_mix_post"][i])
        x = _mlp(x, p["norm_mlp_pre"][i], p["w_up"][i], p["w_down"][i], p["norm_mlp_post"][i])
    return x.reshape(b, s, d)


def kernel(x_prompt, x_sample, norm_mix_pre, norm_mix_post, norm_mlp_pre, norm_mlp_post,
           w_in_ab, w_out_ab, a_vnorm, a_ws, a_bs, b_lambda, b_subnorm,
           w_in_c, w_out_c, c_sink, w_up, w_down):
    p = _prep_weights(dict(
        norm_mix_pre=norm_mix_pre, norm_mix_post=norm_mix_post,
        norm_mlp_pre=norm_mlp_pre, norm_mlp_post=norm_mlp_post,
        w_in_ab=w_in_ab, w_out_ab=w_out_ab, a_vnorm=a_vnorm, a_ws=a_ws, a_bs=a_bs,
        b_lambda=b_lambda, b_subnorm=b_subnorm, w_in_c=w_in_c, w_out_c=w_out_c,
        c_sink=c_sink, w_up=w_up, w_down=w_down))
    return (_trunk(x_prompt, p), _trunk(x_sample, p))
```

```python
import functools
import math

import jax
import jax.numpy as jnp
from jax import lax
from jax.experimental import pallas as pl
from jax.experimental.pallas import tpu as pltpu

F32 = jnp.float32
BF16 = jnp.bfloat16

D_MODEL = 2048
DEPTH = 4
HEAD_DIM = 128
CHUNK = 128
A_GROUPS = 8
A_WIDTH = 1024
B_HEADS = 8
B_QK_DIM = 64
B_QK_WIDTH = 1024
B_WIDTH = 1024
AB_IN = 5120
C_HEADS = 16
C_KV_HEADS = 4
C_GROUP = 4
C_WINDOW = 128
C_IN = 3072
D_FF = 8192
ROPE_THETA = 500000.0
ROPE_FRACTION = 4
EPS = 1e-6
LOG2E = 1.4426950408889634
NEG_BIG = -1e30

LANES = 128
VMEM_LIMIT = 56 * 1024 * 1024


def _params(sem, limit=VMEM_LIMIT, flags=None):
    return pltpu.CompilerParams(dimension_semantics=sem, vmem_limit_bytes=limit, flags=flags)


def _weight_spec(block, index_map, resident):
    if resident:
        return pl.BlockSpec(block, index_map, pipeline_mode=pl.Buffered(1))
    return pl.BlockSpec(block, index_map)


def _rms(x, g):
    return x * lax.rsqrt(jnp.mean(x * x, axis=-1, keepdims=True) + EPS) * g


def _in_proj_kernel(x_ref, g_ref, w_ref, cos_ref, sa_ref, sb_ref, o_ref, h_ref, *,
                    rope_lo, rope_hi, half, bn, n_blocks):
    j = pl.program_id(1)
    chunks = bn // LANES

    @pl.when(j == 0)
    def _():
        h_ref[...] = _rms(x_ref[...], g_ref[...]).astype(BF16)

    def project(lo, hi):
        z = jnp.dot(h_ref[...], w_ref[...], preferred_element_type=F32)
        if lo > 0:
            o_ref[:, :lo * LANES] = z[:, :lo * LANES].astype(BF16)
        if hi > lo:
            cos = cos_ref[...]
            sa = sa_ref[...]
            sb = sb_ref[...]
        for c in range(lo, hi):
            zc = z[:, c * LANES:(c + 1) * LANES]
            zr = (zc * cos + pltpu.roll(zc, half, 1) * sa
                  + pltpu.roll(zc, LANES - half, 1) * sb)
            o_ref[:, c * LANES:(c + 1) * LANES] = zr.astype(BF16)
        if hi < chunks:
            o_ref[:, hi * LANES:] = z[:, hi * LANES:].astype(BF16)

    ranges = {}
    for blk in range(n_blocks):
        lo = min(max(rope_lo - blk * chunks, 0), chunks)
        hi = min(max(rope_hi - blk * chunks, lo), chunks)
        ranges.setdefault((lo, hi) if hi > lo else (0, 0), []).append(blk)
    for (lo, hi), blks in ranges.items():
        cond = functools.reduce(jnp.logical_or, [j == blk for blk in blks])
        pl.when(cond)(functools.partial(project, lo, hi))


def _in_proj(x, g, w, tables, seq, *, bn, rope_cols, half, bm=1024):
    m, d = x.shape
    n = w.shape[1]
    bm = min(bm, seq)
    nsb = seq // bm
    cos, sa, sb = tables
    assert rope_cols[0] % LANES == 0 and rope_cols[1] % LANES == 0 and seq % bm == 0 and n % bn == 0
    kern = functools.partial(_in_proj_kernel, rope_lo=rope_cols[0] // LANES,
                             rope_hi=rope_cols[1] // LANES, half=half, bn=bn, n_blocks=n // bn)
    tab_spec = pl.BlockSpec((bm, LANES), lambda i, j: (i % nsb, 0))
    return pl.pallas_call(
        kern,
        grid=(m // bm, n // bn),
        in_specs=[
            pl.BlockSpec((bm, d), lambda i, j: (i, 0)),
            pl.BlockSpec((1, d), lambda i, j: (0, 0)),
            _weight_spec((d, bn), lambda i, j: (0, j), resident=(n == bn)),
            tab_spec, tab_spec, tab_spec,
        ],
        out_specs=pl.BlockSpec((bm, bn), lambda i, j: (i, j)),
        out_shape=jax.ShapeDtypeStruct((m, n), BF16),
        scratch_shapes=[pltpu.VMEM((bm, d), BF16)],
        compiler_params=_params(("parallel", "arbitrary")),
        name="in_proj",
    )(x, g, w, cos, sa, sb)


def _rope_tables(seq, head_dim, rot):
    half = rot // 2
    inv = ROPE_THETA ** (-jnp.arange(half, dtype=F32) / half)
    ang = jnp.arange(seq, dtype=F32)[:, None] * inv[None, :]
    c, s = jnp.cos(ang), jnp.sin(ang)
    z_half = jnp.zeros((seq, half), F32)
    z_rest = jnp.zeros((seq, head_dim - rot), F32)
    cos = jnp.concatenate([c, c, jnp.ones((seq, head_dim - rot), F32)], axis=1)
    sa = jnp.concatenate([z_half, s, z_rest], axis=1)
    sb = jnp.concatenate([-s, z_half, z_rest], axis=1)
    reps = LANES // head_dim
    return tuple(jnp.tile(t, (1, reps)) for t in (cos, sa, sb))


def _gmlp_kernel(u_ref, v_ref, vn_ref, ws_ref, bs_ref, o_ref, *, tm):
    for g in range(A_GROUPS):
        cols = slice(g * HEAD_DIM, (g + 1) * HEAD_DIM)
        u = jax.nn.gelu(u_ref[:, cols].astype(F32))
        v = jax.nn.gelu(v_ref[:, cols].astype(F32))
        v = _rms(v, vn_ref[g:g + 1, :]).astype(BF16)
        w = ws_ref[g]
        b = bs_ref[g]
        for c in range(tm // CHUNK):
            rows = slice(c * CHUNK, (c + 1) * CHUNK)
            mixed = jnp.dot(w, v[rows, :], preferred_element_type=F32) + b
            o_ref[rows, cols] = (u[rows, :] * mixed).astype(BF16)


def _gmlp(z, vnorm, ws, bs_b, *, tm=512):
    m = z.shape[0]
    tm = min(tm, m)
    return pl.pallas_call(
        functools.partial(_gmlp_kernel, tm=tm),
        grid=(m // tm,),
        in_specs=[
            pl.BlockSpec((tm, A_WIDTH), lambda i: (i, 0)),
            pl.BlockSpec((tm, A_WIDTH), lambda i: (i, 1)),
            pl.BlockSpec((A_GROUPS, HEAD_DIM), lambda i: (0, 0)),
            pl.BlockSpec((A_GROUPS, CHUNK, CHUNK), lambda i: (0, 0, 0)),
            pl.BlockSpec((A_GROUPS, CHUNK, HEAD_DIM), lambda i: (0, 0, 0)),
        ],
        out_specs=pl.BlockSpec((tm, A_WIDTH), lambda i: (i, 0)),
        out_shape=jax.ShapeDtypeStruct((m, A_WIDTH), BF16),
        compiler_params=_params(("parallel",)),
        name="gmlp",
    )(z, z, vnorm, ws, bs_b)


BF16_SUBLANES = 16
VT_ROWS = HEAD_DIM + BF16_SUBLANES

def _vt_kernel(v_ref, o_ref, *, tk):
    row = lax.broadcasted_iota(jnp.int32, (BF16_SUBLANES, tk), 0)
    ones_row = jnp.where(row == 0, 1.0, 0.0).astype(BF16)
    for h in range(B_HEADS):
        v = v_ref[:, h * HEAD_DIM:(h + 1) * HEAD_DIM].astype(F32)
        o_ref[h, :HEAD_DIM, :] = v.T.astype(BF16)
        o_ref[h, HEAD_DIM:, :] = ones_row


def _v_transposed(z3, *, tk):
    b, s, _ = z3.shape
    v_block = (2 * A_WIDTH + 2 * B_QK_WIDTH) // B_WIDTH
    return pl.pallas_call(
        functools.partial(_vt_kernel, tk=tk),
        grid=(b, s // tk),
        in_specs=[pl.BlockSpec((None, tk, B_WIDTH), lambda i, c: (i, c, v_block))],
        out_specs=pl.BlockSpec((None, B_HEADS, None, VT_ROWS, tk), lambda i, c: (i, 0, c, 0, 0)),
        out_shape=jax.ShapeDtypeStruct((b, B_HEADS, s // tk, VT_ROWS, tk), BF16),
        compiler_params=_params(("parallel", "parallel")),
        name="v_transpose",
    )(z3)


def _diff_attn_kernel(q_ref, k_ref, vt_ref, lam_ref, g_ref, o_ref,
                      qt_ref, s_ref, p_ref, a_ref, acc_ref, *, tq, tk, nck, lam_init):
    q = q_ref[...].astype(F32) * (B_QK_DIM ** -0.5 * LOG2E)
    qt = q.T
    row = lax.broadcasted_iota(jnp.int32, qt.shape, 0)
    qt_ref[:, :tq] = jnp.where(row < B_QK_DIM, qt, 0.0).astype(BF16)
    qt_ref[:, tq:] = jnp.where(row >= B_QK_DIM, qt, 0.0).astype(BF16)

    def scores(c):
        start = c * tk if isinstance(c, int) else pl.multiple_of(c * tk, tk)
        return jnp.dot(k_ref[pl.ds(start, tk), :], qt_ref[...], preferred_element_type=F32)

    def softmax(s, m_old):
        m_new = jnp.maximum(m_old, jnp.max(s, axis=0, keepdims=True))
        alpha = jnp.exp2(m_old - m_new)
        p = jnp.exp2(s - m_new).astype(BF16)
        return p, alpha, m_new

    def accumulate(c, p, alpha):
        acc_ref[...] = alpha * acc_ref[...] + jnp.dot(vt_ref[c], p, preferred_element_type=F32)

    def stage(c, slot, m, with_scores, with_values):
        if with_scores:
            s_ref[1 - slot] = scores(c + 1)
        p, alpha, m = softmax(s_ref[slot], m)
        if with_values:
            accumulate(c - 1, p_ref[1 - slot], a_ref[1 - slot])
        p_ref[slot] = p
        a_ref[slot] = alpha
        return m

    def two_stages(j, m):
        m = stage(2 * j + 1, 1, m, True, True)
        return stage(2 * j + 2, 0, m, True, True)

    s_ref[0] = scores(0)
    acc_ref[...] = jnp.zeros(acc_ref.shape, F32)
    m = jnp.full((1, 2 * tq), NEG_BIG, F32)
    m = stage(0, 0, m, True, False)
    m = lax.fori_loop(0, (nck - 2) // 2, two_stages, m)
    m = stage(nck - 1, 1, m, False, True)
    accumulate(nck - 1, p_ref[1], a_ref[1])

    lv = lam_ref[...]
    lam = (jnp.exp(jnp.sum(lv[0:1] * lv[1:2], axis=1, keepdims=True))
           - jnp.exp(jnp.sum(lv[2:3] * lv[3:4], axis=1, keepdims=True)) + lam_init)
    inv_l = 1.0 / acc_ref[HEAD_DIM:HEAD_DIM + 1, :]
    acc = acc_ref[:HEAD_DIM, :] * inv_l
    ot = acc[:, :tq] - lam * acc[:, tq:]
    o = ot.T
    o_ref[...] = (_rms(o, g_ref[...]) * (1.0 - lam_init)).astype(BF16)


def _diff_attn(z3, vt, lam_vecs, subnorm, lam_init, *, tq, tk):
    b, s, _ = z3.shape
    nck = s // tk
    assert s % tq == 0 and s % tk == 0 and nck % 2 == 0
    q_block = 2 * A_WIDTH // LANES
    k_block = (2 * A_WIDTH + B_QK_WIDTH) // LANES
    kern = functools.partial(_diff_attn_kernel, tq=tq, tk=tk, nck=nck, lam_init=lam_init)
    return pl.pallas_call(
        kern,
        grid=(b, B_HEADS, s // tq),
        in_specs=[
            pl.BlockSpec((None, tq, LANES), lambda i, h, t: (i, t, q_block + h)),
            pl.BlockSpec((None, s, LANES), lambda i, h, t: (i, 0, k_block + h)),
            pl.BlockSpec((None, None, nck, VT_ROWS, tk), lambda i, h, t: (i, h, 0, 0, 0)),
            pl.BlockSpec((4, B_QK_DIM), lambda i, h, t: (0, 0)),
            pl.BlockSpec((1, HEAD_DIM), lambda i, h, t: (0, 0)),
        ],
        out_specs=pl.BlockSpec((None, tq, HEAD_DIM), lambda i, h, t: (i, t, h)),
        out_shape=jax.ShapeDtypeStruct((b, s, B_WIDTH), BF16),
        scratch_shapes=[
            pltpu.VMEM((HEAD_DIM, 2 * tq), BF16),
            pltpu.VMEM((2, tk, 2 * tq), F32),
            pltpu.VMEM((2, tk, 2 * tq), BF16),
            pltpu.VMEM((2, 1, 2 * tq), F32),
            pltpu.VMEM((VT_ROWS, 2 * tq), F32),
        ],
        compiler_params=_params(("parallel", "parallel", "arbitrary")),
        name="diff_attn",
    )(z3, z3, vt, lam_vecs, subnorm)


def _win_attn_kernel(sink_ref, q_ref, kp_ref, kc_ref, kn_ref, vp_ref, vc_ref, vn_ref, o_ref, *, nb):
    n = pl.program_id(1)
    rows = C_GROUP * CHUNK
    r = lax.broadcasted_iota(jnp.int32, (rows, 3 * CHUNK), 0) % CHUNK
    c = lax.broadcasted_iota(jnp.int32, (rows, 3 * CHUNK), 1)
    rel = c - CHUNK - r
    valid = jnp.logical_and(rel <= C_WINDOW, rel >= -C_WINDOW)
    valid = jnp.logical_and(valid, jnp.logical_or(n > 0, c >= CHUNK))
    valid = jnp.logical_and(valid, jnp.logical_or(n < nb - 1, c < 2 * CHUNK))
    ones_block = jnp.ones((3 * CHUNK, LANES), BF16)
    scale = HEAD_DIM ** -0.5 * LOG2E
    for kh in range(C_KV_HEADS):
        kcols = slice(kh * HEAD_DIM, (kh + 1) * HEAD_DIM)
        kband = jnp.concatenate([kp_ref[:, kcols], kc_ref[:, kcols], kn_ref[:, kcols]], axis=0)
        vband = jnp.concatenate([vp_ref[:, kcols], vc_ref[:, kcols], vn_ref[:, kcols]], axis=0)
        v_aug = jnp.concatenate([vband, ones_block], axis=1)
        heads = [kh * C_GROUP + g for g in range(C_GROUP)]
        q = jnp.concatenate([q_ref[:, h * HEAD_DIM:(h + 1) * HEAD_DIM] for h in heads], axis=0)
        q = (q.astype(F32) * scale).astype(BF16)
        sink = jnp.concatenate([jnp.full((CHUNK, LANES), sink_ref[h] * LOG2E, F32) for h in heads], axis=0)
        s = lax.dot_general(q, kband, (((1,), (1,)), ((), ())), preferred_element_type=F32)
        s = jnp.where(valid, s, NEG_BIG)
        m = jnp.maximum(jnp.broadcast_to(jnp.max(s, axis=-1, keepdims=True), sink.shape), sink)
        p = jnp.exp2(s - jnp.concatenate([m, m, m], axis=1)).astype(BF16)
        o_aug = jnp.dot(p, v_aug, preferred_element_type=F32)
        denom = o_aug[:, HEAD_DIM:] + jnp.exp2(sink - m)
        o = o_aug[:, :HEAD_DIM] / denom
        for g, h in enumerate(heads):
            o_ref[:, h * HEAD_DIM:(h + 1) * HEAD_DIM] = o[g * CHUNK:(g + 1) * CHUNK, :].astype(BF16)


def _win_attn(z3, sink):
    b, s, _ = z3.shape
    nb = s // CHUNK
    kv_w = C_KV_HEADS * HEAD_DIM
    k_block = C_HEADS * HEAD_DIM // kv_w
    v_block = k_block + 1
    prev = lambda n: jnp.maximum(n - 1, 0)
    nxt = lambda n: jnp.minimum(n + 1, nb - 1)

    def kv_spec(blk, f):
        return pl.BlockSpec((None, CHUNK, kv_w), lambda i, n, sk: (i, f(n), blk))

    same = lambda n: n
    grid_spec = pltpu.PrefetchScalarGridSpec(
        num_scalar_prefetch=1,
        grid=(b, nb),
        in_specs=[
            pl.BlockSpec((None, CHUNK, C_HEADS * HEAD_DIM), lambda i, n, sk: (i, n, 0)),
            kv_spec(k_block, prev), kv_spec(k_block, same), kv_spec(k_block, nxt),
            kv_spec(v_block, prev), kv_spec(v_block, same), kv_spec(v_block, nxt),
        ],
        out_specs=pl.BlockSpec((None, CHUNK, C_HEADS * HEAD_DIM), lambda i, n, sk: (i, n, 0)),
    )
    return pl.pallas_call(
        functools.partial(_win_attn_kernel, nb=nb),
        grid_spec=grid_spec,
        out_shape=jax.ShapeDtypeStruct((b, s, C_HEADS * HEAD_DIM), BF16),
        compiler_params=_params(("parallel", "parallel")),
        name="win_attn",
    )(sink, z3, z3, z3, z3, z3, z3, z3)


def _out_proj_kernel(*refs, n_in):
    a_refs = refs[:n_in]
    w_refs = refs[n_in:2 * n_in]
    x_ref, g_ref, o_ref = refs[2 * n_in:]
    acc = jnp.dot(a_refs[0][...], w_refs[0][...], preferred_element_type=F32)
    for a_ref, w_ref in zip(a_refs[1:], w_refs[1:]):
        acc = acc + jnp.dot(a_ref[...], w_ref[...], preferred_element_type=F32)
    o_ref[...] = x_ref[...] + _rms(acc, g_ref[...])


def _out_proj(acts, ws, x, g, *, bm=512):
    m, d = x.shape
    bm = min(bm, m)
    n_in = len(acts)
    in_specs = [pl.BlockSpec((bm, a.shape[1]), lambda i: (i, 0)) for a in acts]
    in_specs += [pl.BlockSpec(w.shape, lambda i: (0, 0)) for w in ws]
    in_specs += [pl.BlockSpec((bm, d), lambda i: (i, 0)), pl.BlockSpec((1, d), lambda i: (0, 0))]
    return pl.pallas_call(
        functools.partial(_out_proj_kernel, n_in=n_in),
        grid=(m // bm,),
        in_specs=in_specs,
        out_specs=pl.BlockSpec((bm, d), lambda i: (i, 0)),
        out_shape=jax.ShapeDtypeStruct((m, d), F32),
        compiler_params=_params(("parallel",)),
        name="out_proj",
    )(*acts, *ws, x, g)


def _mlp_kernel(x_ref, g1_ref, wu_ref, wd_ref, g2_ref, o_ref, h_ref, acc_ref):
    f = pl.program_id(1)

    @pl.when(f == 0)
    def _():
        h_ref[...] = _rms(x_ref[...], g1_ref[...]).astype(BF16)
        acc_ref[...] = jnp.zeros(acc_ref.shape, F32)

    a = jnp.dot(h_ref[...], wu_ref[...], preferred_element_type=F32)
    a = jnp.square(jnp.maximum(a, 0.0)).astype(BF16)
    acc_ref[...] += jnp.dot(a, wd_ref[...], preferred_element_type=F32)

    @pl.when(f == pl.num_programs(1) - 1)
    def _():
        o_ref[...] = x_ref[...] + _rms(acc_ref[...], g2_ref[...])


def _mlp(x, g1, wu, wd, g2, *, bm=512, tf=1024):
    m, d = x.shape
    dff = wu.shape[1]
    bm = min(bm, m)
    return pl.pallas_call(
        _mlp_kernel,
        grid=(m // bm, dff // tf),
        in_specs=[
            pl.BlockSpec((bm, d), lambda i, f: (i, 0)),
            pl.BlockSpec((1, d), lambda i, f: (0, 0)),
            pl.BlockSpec((d, tf), lambda i, f: (0, f)),
            pl.BlockSpec((tf, d), lambda i, f: (f, 0)),
            pl.BlockSpec((1, d), lambda i, f: (0, 0)),
        ],
        out_specs=pl.BlockSpec((bm, d), lambda i, f: (i, 0)),
        out_shape=jax.ShapeDtypeStruct((m, d), F32),
        scratch_shapes=[pltpu.VMEM((bm, d), BF16), pltpu.VMEM((bm, d), F32)],
        compiler_params=_params(("parallel", "arbitrary")),
        name="mlp",
    )(x, g1, wu, wd, g2)


def _pair_heads(w):
    d = w.shape[0]
    return w.reshape(d, 2, B_HEADS, B_QK_DIM).transpose(0, 2, 1, 3).reshape(d, B_QK_WIDTH)


def _prep_weights(p):
    row = lambda a: a.reshape(a.shape[0], 1, a.shape[-1]).astype(F32)
    w_in_ab = p["w_in_ab"]
    q0, k0, v0 = 2 * A_WIDTH, 2 * A_WIDTH + B_QK_WIDTH, 2 * A_WIDTH + 2 * B_QK_WIDTH
    w_in_ab = jnp.concatenate([
        w_in_ab[:, :, :q0],
        jax.vmap(_pair_heads)(w_in_ab[:, :, q0:k0]),
        jax.vmap(_pair_heads)(w_in_ab[:, :, k0:v0]),
        w_in_ab[:, :, v0:],
    ], axis=-1).astype(BF16)
    return dict(
        norm_mix_pre=row(p["norm_mix_pre"]), norm_mix_post=row(p["norm_mix_post"]),
        norm_mlp_pre=row(p["norm_mlp_pre"]), norm_mlp_post=row(p["norm_mlp_post"]),
        w_in_ab=w_in_ab,
        w_out_ab=p["w_out_ab"].astype(BF16),
        a_vnorm=p["a_vnorm"].astype(F32),
        a_ws=p["a_ws"].astype(BF16),
        a_bs_b=jnp.broadcast_to(p["a_bs"].astype(F32)[..., None],
                                p["a_bs"].shape + (HEAD_DIM,)),
        b_lambda=p["b_lambda"].astype(F32),
        b_subnorm=p["b_subnorm"].astype(F32)[:, None, :],
        w_in_c=p["w_in_c"].astype(BF16),
        w_out_c=p["w_out_c"].astype(BF16),
        c_sink=p["c_sink"].astype(F32),
        w_up=p["w_up"].astype(BF16),
        w_down=p["w_down"].astype(BF16),
    )


def _trunk(x3, p):
    b, s, d = x3.shape
    m = b * s
    x = x3.reshape(m, d)
    tq = min(2048, s)
    tk = min(512, s // 2)
    tab_ab = _rope_tables(s, B_QK_DIM, B_QK_DIM // ROPE_FRACTION)
    tab_c = _rope_tables(s, HEAD_DIM, HEAD_DIM // ROPE_FRACTION)
    for i in range(DEPTH):
        j = i // 2
        if i % 2 == 0:
            lam_init = 0.8 - 0.6 * math.exp(-0.3 * i)
            z = _in_proj(x, p["norm_mix_pre"][i], p["w_in_ab"][j], tab_ab, s, bn=AB_IN, bm=256,
                         rope_cols=(2 * A_WIDTH, 2 * A_WIDTH + 2 * B_QK_WIDTH),
                         half=B_QK_DIM // ROPE_FRACTION // 2)
            a_out = _gmlp(z, p["a_vnorm"][j], p["a_ws"][j], p["a_bs_b"][j])
            z3 = z.reshape(b, s, AB_IN)
            vt = _v_transposed(z3, tk=tk)
            b_out = _diff_attn(z3, vt, p["b_lambda"][j], p["b_subnorm"][j], lam_init, tq=tq, tk=tk)
            w_out = p["w_out_ab"][j]
            x = _out_proj([a_out, b_out.reshape(m, B_WIDTH)], [w_out[:A_WIDTH], w_out[A_WIDTH:]],
                          x, p["norm_mix_post"][i])
        else:
            z = _in_proj(x, p["norm_mix_pre"][i], p["w_in_c"][j], tab_c, s, bn=3072, bm=512,
                         rope_cols=(0, (C_HEADS + C_KV_HEADS) * HEAD_DIM),
                         half=HEAD_DIM // ROPE_FRACTION // 2)
            c_out = _win_attn(z.reshape(b, s, C_IN), p["c_sink"][j])
            x = _out_proj([c_out.reshape(m, D_MODEL)], [p["w_out_c"][j]], x, p["norm_mix_post"][i])
        x = _mlp(x, p["norm_mlp_pre"][i], p["w_up"][i], p["w_down"][i], p["norm_mlp_post"][i])
    return x.reshape(b, s, d)


def kernel(x_prompt, x_sample, norm_mix_pre, norm_mix_post, norm_mlp_pre, norm_mlp_post,
           w_in_ab, w_out_ab, a_vnorm, a_ws, a_bs, b_lambda, b_subnorm,
           w_in_c, w_out_c, c_sink, w_up, w_down):
    p = _prep_weights(dict(
        norm_mix_pre=norm_mix_pre, norm_mix_post=norm_mix_post,
        norm_mlp_pre=norm_mlp_pre, norm_mlp_post=norm_mlp_post,
        w_in_ab=w_in_ab, w_out_ab=w_out_ab, a_vnorm=a_vnorm, a_ws=a_ws, a_bs=a_bs,
        b_lambda=b_lambda, b_subnorm=b_subnorm, w_in_c=w_in_c, w_out_c=w_out_c,
        c_sink=c_sink, w_up=w_up, w_down=w_down))
    return (_trunk(x_prompt, p), _trunk(x_sample, p))
```

```python
import functools
import math

import jax
import jax.numpy as jnp
from jax import lax
from jax.experimental import pallas as pl
from jax.experimental.pallas import tpu as pltpu

F32 = jnp.float32
BF16 = jnp.bfloat16

D_MODEL = 2048
DEPTH = 4
HEAD_DIM = 128
CHUNK = 128
A_GROUPS = 8
A_WIDTH = 1024
B_HEADS = 8
B_QK_DIM = 64
B_QK_WIDTH = 1024
B_WIDTH = 1024
AB_IN = 5120
C_HEADS = 16
C_KV_HEADS = 4
C_GROUP = 4
C_WINDOW = 128
C_IN = 3072
D_FF = 8192
ROPE_THETA = 500000.0
ROPE_FRACTION = 4
EPS = 1e-6
LOG2E = 1.4426950408889634
NEG_BIG = -1e30

LANES = 128
VMEM_LIMIT = 56 * 1024 * 1024


def _params(sem, limit=VMEM_LIMIT, flags=None):
    return pltpu.CompilerParams(dimension_semantics=sem, vmem_limit_bytes=limit, flags=flags)


def _weight_spec(block, index_map, resident):
    if resident:
        return pl.BlockSpec(block, index_map, pipeline_mode=pl.Buffered(1))
    return pl.BlockSpec(block, index_map)


def _rms(x, g):
    return x * lax.rsqrt(jnp.mean(x * x, axis=-1, keepdims=True) + EPS) * g


def _in_proj_kernel(x_ref, g_ref, w_ref, cos_ref, sa_ref, sb_ref, o_ref, *, rope_lo, rope_hi, half):
    h = _rms(x_ref[...], g_ref[...]).astype(BF16)
    z = jnp.dot(h, w_ref[...], preferred_element_type=F32)
    if rope_lo > 0:
        o_ref[:, :rope_lo * LANES] = z[:, :rope_lo * LANES].astype(BF16)
    cos = cos_ref[...]
    sa = sa_ref[...]
    sb = sb_ref[...]
    for c in range(rope_lo, rope_hi):
        zc = z[:, c * LANES:(c + 1) * LANES]
        zr = (zc * cos + pltpu.roll(zc, half, 1) * sa
              + pltpu.roll(zc, LANES - half, 1) * sb)
        o_ref[:, c * LANES:(c + 1) * LANES] = zr.astype(BF16)
    if rope_hi * LANES < o_ref.shape[1]:
        o_ref[:, rope_hi * LANES:] = z[:, rope_hi * LANES:].astype(BF16)


def _in_proj(x, g, w, tables, seq, *, bm, rope_cols, half):
    m, d = x.shape
    n = w.shape[1]
    bm = min(bm, seq)
    nsb = seq // bm
    cos, sa, sb = tables
    assert rope_cols[0] % LANES == 0 and rope_cols[1] % LANES == 0 and seq % bm == 0
    kern = functools.partial(_in_proj_kernel, rope_lo=rope_cols[0] // LANES,
                             rope_hi=rope_cols[1] // LANES, half=half)
    tab_spec = pl.BlockSpec((bm, LANES), lambda i: (i % nsb, 0))
    return pl.pallas_call(
        kern,
        grid=(m // bm,),
        in_specs=[
            pl.BlockSpec((bm, d), lambda i: (i, 0)),
            pl.BlockSpec((1, d), lambda i: (0, 0)),
            _weight_spec((d, n), lambda i: (0, 0), resident=True),
            tab_spec, tab_spec, tab_spec,
        ],
        out_specs=pl.BlockSpec((bm, n), lambda i: (i, 0)),
        out_shape=jax.ShapeDtypeStruct((m, n), BF16),
        compiler_params=_params(("parallel",)),
        name="in_proj",
    )(x, g, w, cos, sa, sb)


def _rope_tables(seq, head_dim, rot):
    half = rot // 2
    inv = ROPE_THETA ** (-jnp.arange(half, dtype=F32) / half)
    ang = jnp.arange(seq, dtype=F32)[:, None] * inv[None, :]
    c, s = jnp.cos(ang), jnp.sin(ang)
    z_half = jnp.zeros((seq, half), F32)
    z_rest = jnp.zeros((seq, head_dim - rot), F32)
    cos = jnp.concatenate([c, c, jnp.ones((seq, head_dim - rot), F32)], axis=1)
    sa = jnp.concatenate([z_half, s, z_rest], axis=1)
    sb = jnp.concatenate([-s, z_half, z_rest], axis=1)
    reps = LANES // head_dim
    return tuple(jnp.tile(t, (1, reps)) for t in (cos, sa, sb))


def _gmlp_kernel(u_ref, v_ref, vn_ref, ws_ref, bs_ref, o_ref, *, tm):
    for g in range(A_GROUPS):
        cols = slice(g * HEAD_DIM, (g + 1) * HEAD_DIM)
        u = jax.nn.gelu(u_ref[:, cols].astype(F32))
        v = jax.nn.gelu(v_ref[:, cols].astype(F32))
        v = _rms(v, vn_ref[g:g + 1, :]).astype(BF16)
        w = ws_ref[g]
        b = bs_ref[g]
        for c in range(tm // CHUNK):
            rows = slice(c * CHUNK, (c + 1) * CHUNK)
            mixed = jnp.dot(w, v[rows, :], preferred_element_type=F32) + b
            o_ref[rows, cols] = (u[rows, :] * mixed).astype(BF16)


def _gmlp(z, vnorm, ws, bs_b, *, tm=512):
    m = z.shape[0]
    tm = min(tm, m)
    return pl.pallas_call(
        functools.partial(_gmlp_kernel, tm=tm),
        grid=(m // tm,),
        in_specs=[
            pl.BlockSpec((tm, A_WIDTH), lambda i: (i, 0)),
            pl.BlockSpec((tm, A_WIDTH), lambda i: (i, 1)),
            pl.BlockSpec((A_GROUPS, HEAD_DIM), lambda i: (0, 0)),
            pl.BlockSpec((A_GROUPS, CHUNK, CHUNK), lambda i: (0, 0, 0)),
            pl.BlockSpec((A_GROUPS, CHUNK, HEAD_DIM), lambda i: (0, 0, 0)),
        ],
        out_specs=pl.BlockSpec((tm, A_WIDTH), lambda i: (i, 0)),
        out_shape=jax.ShapeDtypeStruct((m, A_WIDTH), BF16),
        compiler_params=_params(("parallel",)),
        name="gmlp",
    )(z, z, vnorm, ws, bs_b)


BF16_SUBLANES = 16
VT_ROWS = HEAD_DIM + BF16_SUBLANES

def _vt_kernel(v_ref, o_ref, *, tk):
    row = lax.broadcasted_iota(jnp.int32, (BF16_SUBLANES, tk), 0)
    ones_row = jnp.where(row == 0, 1.0, 0.0).astype(BF16)
    for h in range(B_HEADS):
        v = v_ref[:, h * HEAD_DIM:(h + 1) * HEAD_DIM].astype(F32)
        o_ref[h, :HEAD_DIM, :] = v.T.astype(BF16)
        o_ref[h, HEAD_DIM:, :] = ones_row


def _v_transposed(z3, *, tk):
    b, s, _ = z3.shape
    v_block = (2 * A_WIDTH + 2 * B_QK_WIDTH) // B_WIDTH
    return pl.pallas_call(
        functools.partial(_vt_kernel, tk=tk),
        grid=(b, s // tk),
        in_specs=[pl.BlockSpec((None, tk, B_WIDTH), lambda i, c: (i, c, v_block))],
        out_specs=pl.BlockSpec((None, B_HEADS, None, VT_ROWS, tk), lambda i, c: (i, 0, c, 0, 0)),
        out_shape=jax.ShapeDtypeStruct((b, B_HEADS, s // tk, VT_ROWS, tk), BF16),
        compiler_params=_params(("parallel", "parallel")),
        name="v_transpose",
    )(z3)


def _diff_attn_kernel(q_ref, k_ref, vt_ref, lam_ref, g_ref, o_ref,
                      qt_ref, s_ref, p_ref, a_ref, acc_ref, *, tq, tk, nck, lam_init):
    q = q_ref[...].astype(F32) * (B_QK_DIM ** -0.5 * LOG2E)
    qt = q.T
    row = lax.broadcasted_iota(jnp.int32, qt.shape, 0)
    qt_ref[:, :tq] = jnp.where(row < B_QK_DIM, qt, 0.0).astype(BF16)
    qt_ref[:, tq:] = jnp.where(row >= B_QK_DIM, qt, 0.0).astype(BF16)

    def scores(c):
        start = c * tk if isinstance(c, int) else pl.multiple_of(c * tk, tk)
        return jnp.dot(k_ref[pl.ds(start, tk), :], qt_ref[...], preferred_element_type=F32)

    def softmax(s, m_old):
        m_new = jnp.maximum(m_old, jnp.max(s, axis=0, keepdims=True))
        alpha = jnp.exp2(m_old - m_new)
        p = jnp.exp2(s - m_new).astype(BF16)
        return p, alpha, m_new

    def accumulate(c, p, alpha):
        acc_ref[...] = alpha * acc_ref[...] + jnp.dot(vt_ref[c], p, preferred_element_type=F32)

    def stage(c, slot, m, with_scores, with_values):
        if with_scores:
            s_ref[1 - slot] = scores(c + 1)
        p, alpha, m = softmax(s_ref[slot], m)
        if with_values:
            accumulate(c - 1, p_ref[1 - slot], a_ref[1 - slot])
        p_ref[slot] = p
        a_ref[slot] = alpha
        return m

    def two_stages(j, m):
        m = stage(2 * j + 1, 1, m, True, True)
        return stage(2 * j + 2, 0, m, True, True)

    s_ref[0] = scores(0)
    acc_ref[...] = jnp.zeros(acc_ref.shape, F32)
    m = jnp.full((1, 2 * tq), NEG_BIG, F32)
    m = stage(0, 0, m, True, False)
    m = lax.fori_loop(0, (nck - 2) // 2, two_stages, m)
    m = stage(nck - 1, 1, m, False, True)
    accumulate(nck - 1, p_ref[1], a_ref[1])

    lv = lam_ref[...]
    lam = (jnp.exp(jnp.sum(lv[0:1] * lv[1:2], axis=1, keepdims=True))
           - jnp.exp(jnp.sum(lv[2:3] * lv[3:4], axis=1, keepdims=True)) + lam_init)
    inv_l = 1.0 / acc_ref[HEAD_DIM:HEAD_DIM + 1, :]
    acc = acc_ref[:HEAD_DIM, :] * inv_l
    ot = acc[:, :tq] - lam * acc[:, tq:]
    o = ot.T
    o_ref[...] = (_rms(o, g_ref[...]) * (1.0 - lam_init)).astype(BF16)


def _diff_attn(z3, vt, lam_vecs, subnorm, lam_init, *, tq, tk):
    b, s, _ = z3.shape
    nck = s // tk
    assert s % tq == 0 and s % tk == 0 and nck % 2 == 0
    q_block = 2 * A_WIDTH // LANES
    k_block = (2 * A_WIDTH + B_QK_WIDTH) // LANES
    kern = functools.partial(_diff_attn_kernel, tq=tq, tk=tk, nck=nck, lam_init=lam_init)
    return pl.pallas_call(
        kern,
        grid=(b, B_HEADS, s // tq),
        in_specs=[
            pl.BlockSpec((None, tq, LANES), lambda i, h, t: (i, t, q_block + h)),
            pl.BlockSpec((None, s, LANES), lambda i, h, t: (i, 0, k_block + h)),
            pl.BlockSpec((None, None, nck, VT_ROWS, tk), lambda i, h, t: (i, h, 0, 0, 0)),
            pl.BlockSpec((4, B_QK_DIM), lambda i, h, t: (0, 0)),
            pl.BlockSpec((1, HEAD_DIM), lambda i, h, t: (0, 0)),
        ],
        out_specs=pl.BlockSpec((None, tq, HEAD_DIM), lambda i, h, t: (i, t, h)),
        out_shape=jax.ShapeDtypeStruct((b, s, B_WIDTH), BF16),
        scratch_shapes=[
            pltpu.VMEM((HEAD_DIM, 2 * tq), BF16),
            pltpu.VMEM((2, tk, 2 * tq), F32),
            pltpu.VMEM((2, tk, 2 * tq), BF16),
            pltpu.VMEM((2, 1, 2 * tq), F32),
            pltpu.VMEM((VT_ROWS, 2 * tq), F32),
        ],
        compiler_params=_params(("parallel", "parallel", "arbitrary")),
        name="diff_attn",
    )(z3, z3, vt, lam_vecs, subnorm)


def _win_attn_kernel(sink_ref, q_ref, kp_ref, kc_ref, kn_ref, vp_ref, vc_ref, vn_ref, o_ref, *, nb):
    n = pl.program_id(1)
    rows = C_GROUP * CHUNK
    r = lax.broadcasted_iota(jnp.int32, (rows, 3 * CHUNK), 0) % CHUNK
    c = lax.broadcasted_iota(jnp.int32, (rows, 3 * CHUNK), 1)
    rel = c - CHUNK - r
    valid = jnp.logical_and(rel <= C_WINDOW, rel >= -C_WINDOW)
    valid = jnp.logical_and(valid, jnp.logical_or(n > 0, c >= CHUNK))
    valid = jnp.logical_and(valid, jnp.logical_or(n < nb - 1, c < 2 * CHUNK))
    ones_block = jnp.ones((3 * CHUNK, LANES), BF16)
    scale = HEAD_DIM ** -0.5 * LOG2E
    for kh in range(C_KV_HEADS):
        kcols = slice(kh * HEAD_DIM, (kh + 1) * HEAD_DIM)
        kband = jnp.concatenate([kp_ref[:, kcols], kc_ref[:, kcols], kn_ref[:, kcols]], axis=0)
        vband = jnp.concatenate([vp_ref[:, kcols], vc_ref[:, kcols], vn_ref[:, kcols]], axis=0)
        v_aug = jnp.concatenate([vband, ones_block], axis=1)
        heads = [kh * C_GROUP + g for g in range(C_GROUP)]
        q = jnp.concatenate([q_ref[:, h * HEAD_DIM:(h + 1) * HEAD_DIM] for h in heads], axis=0)
        q = (q.astype(F32) * scale).astype(BF16)
        sink = jnp.concatenate([jnp.full((CHUNK, LANES), sink_ref[h] * LOG2E, F32) for h in heads], axis=0)
        s = lax.dot_general(q, kband, (((1,), (1,)), ((), ())), preferred_element_type=F32)
        s = jnp.where(valid, s, NEG_BIG)
        m = jnp.maximum(jnp.broadcast_to(jnp.max(s, axis=-1, keepdims=True), sink.shape), sink)
        p = jnp.exp2(s - jnp.concatenate([m, m, m], axis=1)).astype(BF16)
        o_aug = jnp.dot(p, v_aug, preferred_element_type=F32)
        denom = o_aug[:, HEAD_DIM:] + jnp.exp2(sink - m)
        o = o_aug[:, :HEAD_DIM] / denom
        for g, h in enumerate(heads):
            o_ref[:, h * HEAD_DIM:(h + 1) * HEAD_DIM] = o[g * CHUNK:(g + 1) * CHUNK, :].astype(BF16)


def _win_attn(z3, sink):
    b, s, _ = z3.shape
    nb = s // CHUNK
    kv_w = C_KV_HEADS * HEAD_DIM
    k_block = C_HEADS * HEAD_DIM // kv_w
    v_block = k_block + 1
    prev = lambda n: jnp.maximum(n - 1, 0)
    nxt = lambda n: jnp.minimum(n + 1, nb - 1)

    def kv_spec(blk, f):
        return pl.BlockSpec((None, CHUNK, kv_w), lambda i, n, sk: (i, f(n), blk))

    same = lambda n: n
    grid_spec = pltpu.PrefetchScalarGridSpec(
        num_scalar_prefetch=1,
        grid=(b, nb),
        in_specs=[
            pl.BlockSpec((None, CHUNK, C_HEADS * HEAD_DIM), lambda i, n, sk: (i, n, 0)),
            kv_spec(k_block, prev), kv_spec(k_block, same), kv_spec(k_block, nxt),
            kv_spec(v_block, prev), kv_spec(v_block, same), kv_spec(v_block, nxt),
        ],
        out_specs=pl.BlockSpec((None, CHUNK, C_HEADS * HEAD_DIM), lambda i, n, sk: (i, n, 0)),
    )
    return pl.pallas_call(
        functools.partial(_win_attn_kernel, nb=nb),
        grid_spec=grid_spec,
        out_shape=jax.ShapeDtypeStruct((b, s, C_HEADS * HEAD_DIM), BF16),
        compiler_params=_params(("parallel", "parallel")),
        name="win_attn",
    )(sink, z3, z3, z3, z3, z3, z3, z3)


def _out_proj_kernel(*refs, n_in):
    a_refs = refs[:n_in]
    w_refs = refs[n_in:2 * n_in]
    x_ref, g_ref, o_ref = refs[2 * n_in:]
    acc = jnp.dot(a_refs[0][...], w_refs[0][...], preferred_element_type=F32)
    for a_ref, w_ref in zip(a_refs[1:], w_refs[1:]):
        acc = acc + jnp.dot(a_ref[...], w_ref[...], preferred_element_type=F32)
    o_ref[...] = x_ref[...] + _rms(acc, g_ref[...])


def _out_proj(acts, ws, x, g, *, bm=512):
    m, d = x.shape
    bm = min(bm, m)
    n_in = len(acts)
    in_specs = [pl.BlockSpec((bm, a.shape[1]), lambda i: (i, 0)) for a in acts]
    in_specs += [_weight_spec(w.shape, lambda i: (0, 0), resident=True) for w in ws]
    in_specs += [pl.BlockSpec((bm, d), lambda i: (i, 0)), pl.BlockSpec((1, d), lambda i: (0, 0))]
    return pl.pallas_call(
        functools.partial(_out_proj_kernel, n_in=n_in),
        grid=(m // bm,),
        in_specs=in_specs,
        out_specs=pl.BlockSpec((bm, d), lambda i: (i, 0)),
        out_shape=jax.ShapeDtypeStruct((m, d), F32),
        compiler_params=_params(("parallel",)),
        name="out_proj",
    )(*acts, *ws, x, g)


def _mlp_kernel(x_ref, g1_ref, wu_ref, wd_ref, g2_ref, o_ref, h_ref, acc_ref):
    f = pl.program_id(1)
    last_f = pl.num_programs(1) - 1

    def step(first, last):
        if first:
            h = _rms(x_ref[...], g1_ref[...]).astype(BF16)
            h_ref[...] = h
        else:
            h = h_ref[...]
        a = jnp.dot(h, wu_ref[...], preferred_element_type=F32)
        a = jnp.square(jnp.maximum(a, 0.0)).astype(BF16)
        part = jnp.dot(a, wd_ref[...], preferred_element_type=F32)
        total = part if first else acc_ref[...] + part
        if last:
            o_ref[...] = x_ref[...] + _rms(total, g2_ref[...])
        else:
            acc_ref[...] = total

    pl.when(f == 0)(functools.partial(step, True, False))
    pl.when(jnp.logical_and(f > 0, f < last_f))(functools.partial(step, False, False))
    pl.when(f == last_f)(functools.partial(step, False, True))


def _mlp(x, g1, wu, wd, g2, *, bm=512, tf=1024):
    m, d = x.shape
    dff = wu.shape[1]
    bm = min(bm, m)
    assert m % bm == 0 and dff % tf == 0 and dff // tf >= 2
    return pl.pallas_call(
        _mlp_kernel,
        grid=(m // bm, dff // tf),
        in_specs=[
            pl.BlockSpec((bm, d), lambda i, f: (i, 0)),
            pl.BlockSpec((1, d), lambda i, f: (0, 0)),
            pl.BlockSpec((d, tf), lambda i, f: (0, f)),
            pl.BlockSpec((tf, d), lambda i, f: (f, 0)),
            pl.BlockSpec((1, d), lambda i, f: (0, 0)),
        ],
        out_specs=pl.BlockSpec((bm, d), lambda i, f: (i, 0)),
        out_shape=jax.ShapeDtypeStruct((m, d), F32),
        scratch_shapes=[pltpu.VMEM((bm, d), BF16), pltpu.VMEM((bm, d), F32)],
        compiler_params=_params(("parallel", "arbitrary")),
        name="mlp",
    )(x, g1, wu, wd, g2)


def _pair_heads(w):
    d = w.shape[0]
    return w.reshape(d, 2, B_HEADS, B_QK_DIM).transpose(0, 2, 1, 3).reshape(d, B_QK_WIDTH)


def _prep_weights(p):
    row = lambda a: a.reshape(a.shape[0], 1, a.shape[-1]).astype(F32)
    w_in_ab = p["w_in_ab"]
    q0, k0, v0 = 2 * A_WIDTH, 2 * A_WIDTH + B_QK_WIDTH, 2 * A_WIDTH + 2 * B_QK_WIDTH
    w_in_ab = jnp.concatenate([
        w_in_ab[:, :, :q0],
        jax.vmap(_pair_heads)(w_in_ab[:, :, q0:k0]),
        jax.vmap(_pair_heads)(w_in_ab[:, :, k0:v0]),
        w_in_ab[:, :, v0:],
    ], axis=-1).astype(BF16)
    return dict(
        norm_mix_pre=row(p["norm_mix_pre"]), norm_mix_post=row(p["norm_mix_post"]),
        norm_mlp_pre=row(p["norm_mlp_pre"]), norm_mlp_post=row(p["norm_mlp_post"]),
        w_in_ab=w_in_ab,
        w_out_ab=p["w_out_ab"].astype(BF16),
        a_vnorm=p["a_vnorm"].astype(F32),
        a_ws=p["a_ws"].astype(BF16),
        a_bs_b=jnp.broadcast_to(p["a_bs"].astype(F32)[..., None],
                                p["a_bs"].shape + (HEAD_DIM,)),
        b_lambda=p["b_lambda"].astype(F32),
        b_subnorm=p["b_subnorm"].astype(F32)[:, None, :],
        w_in_c=p["w_in_c"].astype(BF16),
        w_out_c=p["w_out_c"].astype(BF16),
        c_sink=p["c_sink"].astype(F32),
        w_up=p["w_up"].astype(BF16),
        w_down=p["w_down"].astype(BF16),
    )


def _trunk(x3, p):
    b, s, d = x3.shape
    m = b * s
    x = x3.reshape(m, d)
    tq = min(2048, s)
    tk = min(512, s // 2)
    tab_ab = _rope_tables(s, B_QK_DIM, B_QK_DIM // ROPE_FRACTION)
    tab_c = _rope_tables(s, HEAD_DIM, HEAD_DIM // ROPE_FRACTION)
    for i in range(DEPTH):
        j = i // 2
        if i % 2 == 0:
            lam_init = 0.8 - 0.6 * math.exp(-0.3 * i)
            z = _in_proj(x, p["norm_mix_pre"][i], p["w_in_ab"][j], tab_ab, s, bm=256,
                         rope_cols=(2 * A_WIDTH, 2 * A_WIDTH + 2 * B_QK_WIDTH),
                         half=B_QK_DIM // ROPE_FRACTION // 2)
            a_out = _gmlp(z, p["a_vnorm"][j], p["a_ws"][j], p["a_bs_b"][j])
            z3 = z.reshape(b, s, AB_IN)
            vt = _v_transposed(z3, tk=tk)
            b_out = _diff_attn(z3, vt, p["b_lambda"][j], p["b_subnorm"][j], lam_init, tq=tq, tk=tk)
            w_out = p["w_out_ab"][j]
            x = _out_proj([a_out, b_out.reshape(m, B_WIDTH)], [w_out[:A_WIDTH], w_out[A_WIDTH:]],
                          x, p["norm_mix_post"][i])
        else:
            z = _in_proj(x, p["norm_mix_pre"][i], p["w_in_c"][j], tab_c, s, bm=512,
                         rope_cols=(0, (C_HEADS + C_KV_HEADS) * HEAD_DIM),
                         half=HEAD_DIM // ROPE_FRACTION // 2)
            c_out = _win_attn(z.reshape(b, s, C_IN), p["c_sink"][j])
            x = _out_proj([c_out.reshape(m, D_MODEL)], [p["w_out_c"][j]], x, p["norm_mix_post"][i])
        x = _mlp(x, p["norm_mlp_pre"][i], p["w_up"][i], p["w_down"][i], p["norm_mlp_post"][i])
    return x.reshape(b, s, d)


def kernel(x_prompt, x_sample, norm_mix_pre, norm_mix_post, norm_mlp_pre, norm_mlp_post,
           w_in_ab, w_out_ab, a_vnorm, a_ws, a_bs, b_lambda, b_subnorm,
           w_in_c, w_out_c, c_sink, w_up, w_down):
    p = _prep_weights(dict(
        norm_mix_pre=norm_mix_pre, norm_mix_post=norm_mix_post,
        norm_mlp_pre=norm_mlp_pre, norm_mlp_post=norm_mlp_post,
        w_in_ab=w_in_ab, w_out_ab=w_out_ab, a_vnorm=a_vnorm, a_ws=a_ws, a_bs=a_bs,
        b_lambda=b_lambda, b_subnorm=b_subnorm, w_in_c=w_in_c, w_out_c=w_out_c,
        c_sink=c_sink, w_up=w_up, w_down=w_down))
    return (_trunk(x_prompt, p), _trunk(x_sample, p))
```

```python
import functools
import math

import jax
import jax.numpy as jnp
from jax import lax
from jax.experimental import pallas as pl
from jax.experimental.pallas import tpu as pltpu

F32 = jnp.float32
BF16 = jnp.bfloat16

D_MODEL = 2048
DEPTH = 4
HEAD_DIM = 128
CHUNK = 128
A_GROUPS = 8
A_WIDTH = 1024
B_HEADS = 8
B_QK_DIM = 64
B_QK_WIDTH = 1024
B_WIDTH = 1024
AB_IN = 5120
C_HEADS = 16
C_KV_HEADS = 4
C_GROUP = 4
C_WINDOW = 128
C_IN = 3072
D_FF = 8192
ROPE_THETA = 500000.0
ROPE_FRACTION = 4
EPS = 1e-6
LOG2E = 1.4426950408889634
NEG_BIG = -1e30

LANES = 128
VMEM_LIMIT = 56 * 1024 * 1024

IN_PROJ_AB_ROWS = 256
IN_PROJ_C_ROWS = 512
GMLP_ROWS = 512
OUT_PROJ_ROWS = 512
MLP_ROWS = 512
MLP_HIDDEN_CHUNK = 1024
ATTN_QUERY_TILE = 2048
ATTN_KEY_CHUNK = 512


def _params(sem):
    return pltpu.CompilerParams(dimension_semantics=sem, vmem_limit_bytes=VMEM_LIMIT)


def _resident_spec(block, index_map):
    return pl.BlockSpec(block, index_map, pipeline_mode=pl.Buffered(1))


def _rms(x, g):
    return x * lax.rsqrt(jnp.mean(x * x, axis=-1, keepdims=True) + EPS) * g


def _in_proj_kernel(x_ref, g_ref, w_ref, cos_ref, sa_ref, sb_ref, o_ref, *, rope_lo, rope_hi, half):
    h = _rms(x_ref[...], g_ref[...]).astype(BF16)
    z = jnp.dot(h, w_ref[...], preferred_element_type=F32)
    if rope_lo > 0:
        o_ref[:, :rope_lo * LANES] = z[:, :rope_lo * LANES].astype(BF16)
    cos = cos_ref[...]
    sa = sa_ref[...]
    sb = sb_ref[...]
    for c in range(rope_lo, rope_hi):
        zc = z[:, c * LANES:(c + 1) * LANES]
        zr = (zc * cos + pltpu.roll(zc, half, 1) * sa
              + pltpu.roll(zc, LANES - half, 1) * sb)
        o_ref[:, c * LANES:(c + 1) * LANES] = zr.astype(BF16)
    if rope_hi * LANES < o_ref.shape[1]:
        o_ref[:, rope_hi * LANES:] = z[:, rope_hi * LANES:].astype(BF16)


def _in_proj(x, g, w, tables, seq, *, bm, rope_cols, half):
    m, d = x.shape
    n = w.shape[1]
    bm = min(bm, seq)
    nsb = seq // bm
    cos, sa, sb = tables
    assert rope_cols[0] % LANES == 0 and rope_cols[1] % LANES == 0 and seq % bm == 0
    kern = functools.partial(_in_proj_kernel, rope_lo=rope_cols[0] // LANES,
                             rope_hi=rope_cols[1] // LANES, half=half)
    tab_spec = pl.BlockSpec((bm, LANES), lambda i: (i % nsb, 0))
    return pl.pallas_call(
        kern,
        grid=(m // bm,),
        in_specs=[
            pl.BlockSpec((bm, d), lambda i: (i, 0)),
            pl.BlockSpec((1, d), lambda i: (0, 0)),
            _resident_spec((d, n), lambda i: (0, 0)),
            tab_spec, tab_spec, tab_spec,
        ],
        out_specs=pl.BlockSpec((bm, n), lambda i: (i, 0)),
        out_shape=jax.ShapeDtypeStruct((m, n), BF16),
        compiler_params=_params(("parallel",)),
        name="in_proj",
    )(x, g, w, cos, sa, sb)


def _rope_tables(seq, head_dim, rot):
    half = rot // 2
    inv = ROPE_THETA ** (-jnp.arange(half, dtype=F32) / half)
    ang = jnp.arange(seq, dtype=F32)[:, None] * inv[None, :]
    c, s = jnp.cos(ang), jnp.sin(ang)
    z_half = jnp.zeros((seq, half), F32)
    z_rest = jnp.zeros((seq, head_dim - rot), F32)
    cos = jnp.concatenate([c, c, jnp.ones((seq, head_dim - rot), F32)], axis=1)
    sa = jnp.concatenate([z_half, s, z_rest], axis=1)
    sb = jnp.concatenate([-s, z_half, z_rest], axis=1)
    reps = LANES // head_dim
    return tuple(jnp.tile(t, (1, reps)) for t in (cos, sa, sb))


def _gmlp_kernel(u_ref, v_ref, vn_ref, ws_ref, bs_ref, o_ref, *, tm):
    for g in range(A_GROUPS):
        cols = slice(g * HEAD_DIM, (g + 1) * HEAD_DIM)
        u = jax.nn.gelu(u_ref[:, cols].astype(F32))
        v = jax.nn.gelu(v_ref[:, cols].astype(F32))
        v = _rms(v, vn_ref[g:g + 1, :]).astype(BF16)
        w = ws_ref[g]
        b = bs_ref[g]
        for c in range(tm // CHUNK):
            rows = slice(c * CHUNK, (c + 1) * CHUNK)
            mixed = jnp.dot(w, v[rows, :], preferred_element_type=F32) + b
            o_ref[rows, cols] = (u[rows, :] * mixed).astype(BF16)


def _gmlp(z, vnorm, ws, bs_b, *, tm=GMLP_ROWS):
    m = z.shape[0]
    tm = min(tm, m)
    return pl.pallas_call(
        functools.partial(_gmlp_kernel, tm=tm),
        grid=(m // tm,),
        in_specs=[
            pl.BlockSpec((tm, A_WIDTH), lambda i: (i, 0)),
            pl.BlockSpec((tm, A_WIDTH), lambda i: (i, 1)),
            pl.BlockSpec((A_GROUPS, HEAD_DIM), lambda i: (0, 0)),
            pl.BlockSpec((A_GROUPS, CHUNK, CHUNK), lambda i: (0, 0, 0)),
            pl.BlockSpec((A_GROUPS, CHUNK, HEAD_DIM), lambda i: (0, 0, 0)),
        ],
        out_specs=pl.BlockSpec((tm, A_WIDTH), lambda i: (i, 0)),
        out_shape=jax.ShapeDtypeStruct((m, A_WIDTH), BF16),
        compiler_params=_params(("parallel",)),
        name="gmlp",
    )(z, z, vnorm, ws, bs_b)


BF16_SUBLANES = 16
VT_ROWS = HEAD_DIM + BF16_SUBLANES

def _vt_kernel(v_ref, o_ref, *, tk):
    row = lax.broadcasted_iota(jnp.int32, (BF16_SUBLANES, tk), 0)
    ones_row = jnp.where(row == 0, 1.0, 0.0).astype(BF16)
    for h in range(B_HEADS):
        v = v_ref[:, h * HEAD_DIM:(h + 1) * HEAD_DIM].astype(F32)
        o_ref[h, :HEAD_DIM, :] = v.T.astype(BF16)
        o_ref[h, HEAD_DIM:, :] = ones_row


def _v_transposed(z3, *, tk):
    b, s, _ = z3.shape
    v_block = (2 * A_WIDTH + 2 * B_QK_WIDTH) // B_WIDTH
    return pl.pallas_call(
        functools.partial(_vt_kernel, tk=tk),
        grid=(b, s // tk),
        in_specs=[pl.BlockSpec((None, tk, B_WIDTH), lambda i, c: (i, c, v_block))],
        out_specs=pl.BlockSpec((None, B_HEADS, None, VT_ROWS, tk), lambda i, c: (i, 0, c, 0, 0)),
        out_shape=jax.ShapeDtypeStruct((b, B_HEADS, s // tk, VT_ROWS, tk), BF16),
        compiler_params=_params(("parallel", "parallel")),
        name="v_transpose",
    )(z3)


def _diff_attn_kernel(q_ref, k_ref, vt_ref, lam_ref, g_ref, o_ref,
                      qt_ref, s_ref, p_ref, a_ref, acc_ref, *, tq, tk, nck, lam_init):
    q = q_ref[...].astype(F32) * (B_QK_DIM ** -0.5 * LOG2E)
    qt = q.T
    row = lax.broadcasted_iota(jnp.int32, qt.shape, 0)
    qt_ref[:, :tq] = jnp.where(row < B_QK_DIM, qt, 0.0).astype(BF16)
    qt_ref[:, tq:] = jnp.where(row >= B_QK_DIM, qt, 0.0).astype(BF16)

    def scores(c):
        start = c * tk if isinstance(c, int) else pl.multiple_of(c * tk, tk)
        return jnp.dot(k_ref[pl.ds(start, tk), :], qt_ref[...], preferred_element_type=F32)

    def softmax(s, m_old):
        m_new = jnp.maximum(m_old, jnp.max(s, axis=0, keepdims=True))
        alpha = jnp.exp2(m_old - m_new)
        p = jnp.exp2(s - m_new).astype(BF16)
        return p, alpha, m_new

    def accumulate(c, p, alpha):
        acc_ref[...] = alpha * acc_ref[...] + jnp.dot(vt_ref[c], p, preferred_element_type=F32)

    def stage(c, slot, m, with_scores, with_values):
        if with_scores:
            s_ref[1 - slot] = scores(c + 1)
        p, alpha, m = softmax(s_ref[slot], m)
        if with_values:
            accumulate(c - 1, p_ref[1 - slot], a_ref[1 - slot])
        p_ref[slot] = p
        a_ref[slot] = alpha
        return m

    def two_stages(j, m):
        m = stage(2 * j + 1, 1, m, True, True)
        return stage(2 * j + 2, 0, m, True, True)

    s_ref[0] = scores(0)
    acc_ref[...] = jnp.zeros(acc_ref.shape, F32)
    m = jnp.full((1, 2 * tq), NEG_BIG, F32)
    m = stage(0, 0, m, True, False)
    m = lax.fori_loop(0, (nck - 2) // 2, two_stages, m)
    m = stage(nck - 1, 1, m, False, True)
    accumulate(nck - 1, p_ref[1], a_ref[1])

    lv = lam_ref[...]
    lam = (jnp.exp(jnp.sum(lv[0:1] * lv[1:2], axis=1, keepdims=True))
           - jnp.exp(jnp.sum(lv[2:3] * lv[3:4], axis=1, keepdims=True)) + lam_init)
    inv_l = 1.0 / acc_ref[HEAD_DIM:HEAD_DIM + 1, :]
    acc = acc_ref[:HEAD_DIM, :] * inv_l
    ot = acc[:, :tq] - lam * acc[:, tq:]
    o = ot.T
    o_ref[...] = (_rms(o, g_ref[...]) * (1.0 - lam_init)).astype(BF16)


def _diff_attn(z3, vt, lam_vecs, subnorm, lam_init, *, tq, tk):
    b, s, _ = z3.shape
    nck = s // tk
    assert s % tq == 0 and s % tk == 0 and nck % 2 == 0
    q_block = 2 * A_WIDTH // LANES
    k_block = (2 * A_WIDTH + B_QK_WIDTH) // LANES
    kern = functools.partial(_diff_attn_kernel, tq=tq, tk=tk, nck=nck, lam_init=lam_init)
    return pl.pallas_call(
        kern,
        grid=(b, B_HEADS, s // tq),
        in_specs=[
            pl.BlockSpec((None, tq, LANES), lambda i, h, t: (i, t, q_block + h)),
            pl.BlockSpec((None, s, LANES), lambda i, h, t: (i, 0, k_block + h)),
            pl.BlockSpec((None, None, nck, VT_ROWS, tk), lambda i, h, t: (i, h, 0, 0, 0)),
            pl.BlockSpec((4, B_QK_DIM), lambda i, h, t: (0, 0)),
            pl.BlockSpec((1, HEAD_DIM), lambda i, h, t: (0, 0)),
        ],
        out_specs=pl.BlockSpec((None, tq, HEAD_DIM), lambda i, h, t: (i, t, h)),
        out_shape=jax.ShapeDtypeStruct((b, s, B_WIDTH), BF16),
        scratch_shapes=[
            pltpu.VMEM((HEAD_DIM, 2 * tq), BF16),
            pltpu.VMEM((2, tk, 2 * tq), F32),
            pltpu.VMEM((2, tk, 2 * tq), BF16),
            pltpu.VMEM((2, 1, 2 * tq), F32),
            pltpu.VMEM((VT_ROWS, 2 * tq), F32),
        ],
        compiler_params=_params(("parallel", "parallel", "arbitrary")),
        name="diff_attn",
    )(z3, z3, vt, lam_vecs, subnorm)


def _win_attn_kernel(sink_ref, q_ref, kp_ref, kc_ref, kn_ref, vp_ref, vc_ref, vn_ref, o_ref, *, nb):
    n = pl.program_id(1)
    rows = C_GROUP * CHUNK
    r = lax.broadcasted_iota(jnp.int32, (rows, 3 * CHUNK), 0) % CHUNK
    c = lax.broadcasted_iota(jnp.int32, (rows, 3 * CHUNK), 1)
    rel = c - CHUNK - r
    valid = jnp.logical_and(rel <= C_WINDOW, rel >= -C_WINDOW)
    valid = jnp.logical_and(valid, jnp.logical_or(n > 0, c >= CHUNK))
    valid = jnp.logical_and(valid, jnp.logical_or(n < nb - 1, c < 2 * CHUNK))
    ones_block = jnp.ones((3 * CHUNK, LANES), BF16)
    scale = HEAD_DIM ** -0.5 * LOG2E
    for kh in range(C_KV_HEADS):
        kcols = slice(kh * HEAD_DIM, (kh + 1) * HEAD_DIM)
        kband = jnp.concatenate([kp_ref[:, kcols], kc_ref[:, kcols], kn_ref[:, kcols]], axis=0)
        vband = jnp.concatenate([vp_ref[:, kcols], vc_ref[:, kcols], vn_ref[:, kcols]], axis=0)
        v_aug = jnp.concatenate([vband, ones_block], axis=1)
        heads = [kh * C_GROUP + g for g in range(C_GROUP)]
        q = jnp.concatenate([q_ref[:, h * HEAD_DIM:(h + 1) * HEAD_DIM] for h in heads], axis=0)
        q = (q.astype(F32) * scale).astype(BF16)
        sink = jnp.concatenate([jnp.full((CHUNK, LANES), sink_ref[h] * LOG2E, F32) for h in heads], axis=0)
        s = lax.dot_general(q, kband, (((1,), (1,)), ((), ())), preferred_element_type=F32)
        s = jnp.where(valid, s, NEG_BIG)
        m = jnp.maximum(jnp.broadcast_to(jnp.max(s, axis=-1, keepdims=True), sink.shape), sink)
        p = jnp.exp2(s - jnp.concatenate([m, m, m], axis=1)).astype(BF16)
        o_aug = jnp.dot(p, v_aug, preferred_element_type=F32)
        denom = o_aug[:, HEAD_DIM:] + jnp.exp2(sink - m)
        o = o_aug[:, :HEAD_DIM] / denom
        for g, h in enumerate(heads):
            o_ref[:, h * HEAD_DIM:(h + 1) * HEAD_DIM] = o[g * CHUNK:(g + 1) * CHUNK, :].astype(BF16)


def _win_attn(z3, sink):
    b, s, _ = z3.shape
    nb = s // CHUNK
    kv_w = C_KV_HEADS * HEAD_DIM
    k_block = C_HEADS * HEAD_DIM // kv_w
    v_block = k_block + 1
    prev = lambda n: jnp.maximum(n - 1, 0)
    nxt = lambda n: jnp.minimum(n + 1, nb - 1)

    def kv_spec(blk, f):
        return pl.BlockSpec((None, CHUNK, kv_w), lambda i, n, sk: (i, f(n), blk))

    same = lambda n: n
    grid_spec = pltpu.PrefetchScalarGridSpec(
        num_scalar_prefetch=1,
        grid=(b, nb),
        in_specs=[
            pl.BlockSpec((None, CHUNK, C_HEADS * HEAD_DIM), lambda i, n, sk: (i, n, 0)),
            kv_spec(k_block, prev), kv_spec(k_block, same), kv_spec(k_block, nxt),
            kv_spec(v_block, prev), kv_spec(v_block, same), kv_spec(v_block, nxt),
        ],
        out_specs=pl.BlockSpec((None, CHUNK, C_HEADS * HEAD_DIM), lambda i, n, sk: (i, n, 0)),
    )
    return pl.pallas_call(
        functools.partial(_win_attn_kernel, nb=nb),
        grid_spec=grid_spec,
        out_shape=jax.ShapeDtypeStruct((b, s, C_HEADS * HEAD_DIM), BF16),
        compiler_params=_params(("parallel", "parallel")),
        name="win_attn",
    )(sink, z3, z3, z3, z3, z3, z3, z3)


def _out_proj_kernel(*refs, n_in):
    a_refs = refs[:n_in]
    w_refs = refs[n_in:2 * n_in]
    x_ref, g_ref, o_ref = refs[2 * n_in:]
    acc = jnp.dot(a_refs[0][...], w_refs[0][...], preferred_element_type=F32)
    for a_ref, w_ref in zip(a_refs[1:], w_refs[1:]):
        acc = acc + jnp.dot(a_ref[...], w_ref[...], preferred_element_type=F32)
    o_ref[...] = x_ref[...] + _rms(acc, g_ref[...])


def _out_proj(acts, ws, x, g, *, bm=OUT_PROJ_ROWS):
    m, d = x.shape
    bm = min(bm, m)
    n_in = len(acts)
    in_specs = [pl.BlockSpec((bm, a.shape[1]), lambda i: (i, 0)) for a in acts]
    in_specs += [_resident_spec(w.shape, lambda i: (0, 0)) for w in ws]
    in_specs += [pl.BlockSpec((bm, d), lambda i: (i, 0)), pl.BlockSpec((1, d), lambda i: (0, 0))]
    return pl.pallas_call(
        functools.partial(_out_proj_kernel, n_in=n_in),
        grid=(m // bm,),
        in_specs=in_specs,
        out_specs=pl.BlockSpec((bm, d), lambda i: (i, 0)),
        out_shape=jax.ShapeDtypeStruct((m, d), F32),
        compiler_params=_params(("parallel",)),
        name="out_proj",
    )(*acts, *ws, x, g)


def _mlp_kernel(x_ref, g1_ref, wu_ref, wd_ref, g2_ref, o_ref, h_ref, acc_ref):
    f = pl.program_id(1)
    last_f = pl.num_programs(1) - 1

    def step(first, last):
        if first:
            h = _rms(x_ref[...], g1_ref[...]).astype(BF16)
            h_ref[...] = h
        else:
            h = h_ref[...]
        a = jnp.dot(h, wu_ref[...], preferred_element_type=F32)
        a = jnp.square(jnp.maximum(a, 0.0)).astype(BF16)
        part = jnp.dot(a, wd_ref[...], preferred_element_type=F32)
        total = part if first else acc_ref[...] + part
        if last:
            o_ref[...] = x_ref[...] + _rms(total, g2_ref[...])
        else:
            acc_ref[...] = total

    pl.when(f == 0)(functools.partial(step, True, False))
    pl.when(jnp.logical_and(f > 0, f < last_f))(functools.partial(step, False, False))
    pl.when(f == last_f)(functools.partial(step, False, True))


def _mlp(x, g1, wu, wd, g2, *, bm=MLP_ROWS, tf=MLP_HIDDEN_CHUNK):
    m, d = x.shape
    dff = wu.shape[1]
    bm = min(bm, m)
    assert m % bm == 0 and dff % tf == 0 and dff // tf >= 2
    return pl.pallas_call(
        _mlp_kernel,
        grid=(m // bm, dff // tf),
        in_specs=[
            pl.BlockSpec((bm, d), lambda i, f: (i, 0)),
            pl.BlockSpec((1, d), lambda i, f: (0, 0)),
            pl.BlockSpec((d, tf), lambda i, f: (0, f)),
            pl.BlockSpec((tf, d), lambda i, f: (f, 0)),
            pl.BlockSpec((1, d), lambda i, f: (0, 0)),
        ],
        out_specs=pl.BlockSpec((bm, d), lambda i, f: (i, 0)),
        out_shape=jax.ShapeDtypeStruct((m, d), F32),
        scratch_shapes=[pltpu.VMEM((bm, d), BF16), pltpu.VMEM((bm, d), F32)],
        compiler_params=_params(("parallel", "arbitrary")),
        name="mlp",
    )(x, g1, wu, wd, g2)


def _pair_heads(w):
    d = w.shape[0]
    return w.reshape(d, 2, B_HEADS, B_QK_DIM).transpose(0, 2, 1, 3).reshape(d, B_QK_WIDTH)


def _prep_weights(p):
    row = lambda a: a.reshape(a.shape[0], 1, a.shape[-1]).astype(F32)
    w_in_ab = p["w_in_ab"].astype(BF16)
    q0, k0, v0 = 2 * A_WIDTH, 2 * A_WIDTH + B_QK_WIDTH, 2 * A_WIDTH + 2 * B_QK_WIDTH
    w_in_ab = jnp.concatenate([
        w_in_ab[:, :, :q0],
        jax.vmap(_pair_heads)(w_in_ab[:, :, q0:k0]),
        jax.vmap(_pair_heads)(w_in_ab[:, :, k0:v0]),
        w_in_ab[:, :, v0:],
    ], axis=-1)
    return dict(
        norm_mix_pre=row(p["norm_mix_pre"]), norm_mix_post=row(p["norm_mix_post"]),
        norm_mlp_pre=row(p["norm_mlp_pre"]), norm_mlp_post=row(p["norm_mlp_post"]),
        w_in_ab=w_in_ab,
        w_out_ab=p["w_out_ab"].astype(BF16),
        a_vnorm=p["a_vnorm"].astype(F32),
        a_ws=p["a_ws"].astype(BF16),
        a_bs_b=jnp.broadcast_to(p["a_bs"].astype(F32)[..., None],
                                p["a_bs"].shape + (HEAD_DIM,)),
        b_lambda=p["b_lambda"].astype(F32),
        b_subnorm=p["b_subnorm"].astype(F32)[:, None, :],
        w_in_c=p["w_in_c"].astype(BF16),
        w_out_c=p["w_out_c"].astype(BF16),
        c_sink=p["c_sink"].astype(F32),
        w_up=p["w_up"].astype(BF16),
        w_down=p["w_down"].astype(BF16),
    )


def _trunk(x3, p):
    b, s, d = x3.shape
    m = b * s
    x = x3.reshape(m, d)
    tq = min(ATTN_QUERY_TILE, s)
    tk = min(ATTN_KEY_CHUNK, s // 2)
    tab_ab = _rope_tables(s, B_QK_DIM, B_QK_DIM // ROPE_FRACTION)
    tab_c = _rope_tables(s, HEAD_DIM, HEAD_DIM // ROPE_FRACTION)
    for i in range(DEPTH):
        j = i // 2
        if i % 2 == 0:
            lam_init = 0.8 - 0.6 * math.exp(-0.3 * i)
            z = _in_proj(x, p["norm_mix_pre"][i], p["w_in_ab"][j], tab_ab, s, bm=IN_PROJ_AB_ROWS,
                         rope_cols=(2 * A_WIDTH, 2 * A_WIDTH + 2 * B_QK_WIDTH),
                         half=B_QK_DIM // ROPE_FRACTION // 2)
            a_out = _gmlp(z, p["a_vnorm"][j], p["a_ws"][j], p["a_bs_b"][j])
            z3 = z.reshape(b, s, AB_IN)
            vt = _v_transposed(z3, tk=tk)
            b_out = _diff_attn(z3, vt, p["b_lambda"][j], p["b_subnorm"][j], lam_init, tq=tq, tk=tk)
            w_out = p["w_out_ab"][j]
            x = _out_proj([a_out, b_out.reshape(m, B_WIDTH)], [w_out[:A_WIDTH], w_out[A_WIDTH:]],
                          x, p["norm_mix_post"][i])
        else:
            z = _in_proj(x, p["norm_mix_pre"][i], p["w_in_c"][j], tab_c, s, bm=IN_PROJ_C_ROWS,
                         rope_cols=(0, (C_HEADS + C_KV_HEADS) * HEAD_DIM),
                         half=HEAD_DIM // ROPE_FRACTION // 2)
            c_out = _win_attn(z.reshape(b, s, C_IN), p["c_sink"][j])
            x = _out_proj([c_out.reshape(m, D_MODEL)], [p["w_out_c"][j]], x, p["norm_mix_post"][i])
        x = _mlp(x, p["norm_mlp_pre"][i], p["w_up"][i], p["w_down"][i], p["norm_mlp_post"][i])
    return x.reshape(b, s, d)


def kernel(x_prompt, x_sample, norm_mix_pre, norm_mix_post, norm_mlp_pre, norm_mlp_post,
           w_in_ab, w_out_ab, a_vnorm, a_ws, a_bs, b_lambda, b_subnorm,
           w_in_c, w_out_c, c_sink, w_up, w_down):
    p = _prep_weights(dict(
        norm_mix_pre=norm_mix_pre, norm_mix_post=norm_mix_post,
        norm_mlp_pre=norm_mlp_pre, norm_mlp_post=norm_mlp_post,
        w_in_ab=w_in_ab, w_out_ab=w_out_ab, a_vnorm=a_vnorm, a_ws=a_ws, a_bs=a_bs,
        b_lambda=b_lambda, b_subnorm=b_subnorm, w_in_c=w_in_c, w_out_c=w_out_c,
        c_sink=c_sink, w_up=w_up, w_down=w_down))
    return (_trunk(x_prompt, p), _trunk(x_sample, p))
```

```python
import functools
import math

import jax
import jax.numpy as jnp
from jax import lax
from jax.experimental import pallas as pl
from jax.experimental.pallas import tpu as pltpu

F32 = jnp.float32
BF16 = jnp.bfloat16

D_MODEL = 2048
DEPTH = 4
HEAD_DIM = 128
CHUNK = 128
A_GROUPS = 8
A_WIDTH = 1024
B_HEADS = 8
B_QK_DIM = 64
B_QK_WIDTH = 1024
B_WIDTH = 1024
AB_IN = 5120
C_HEADS = 16
C_KV_HEADS = 4
C_GROUP = 4
C_WINDOW = 128
C_IN = 3072
D_FF = 8192
ROPE_THETA = 500000.0
ROPE_FRACTION = 4
EPS = 1e-6
LOG2E = 1.4426950408889634
NEG_BIG = -1e30

LANES = 128
VMEM_LIMIT = 56 * 1024 * 1024

IN_PROJ_AB_ROWS = 256
IN_PROJ_C_ROWS = 512
OUT_PROJ_ROWS = 512
MLP_ROWS = 512
MLP_HIDDEN_CHUNK = 1024
ATTN_QUERY_TILE = 2048
ATTN_KEY_CHUNK = 512


def _params(sem):
    return pltpu.CompilerParams(dimension_semantics=sem, vmem_limit_bytes=VMEM_LIMIT)


def _resident_spec(block, index_map):
    return pl.BlockSpec(block, index_map, pipeline_mode=pl.Buffered(1))


def _rms(x, g):
    return x * lax.rsqrt(jnp.mean(x * x, axis=-1, keepdims=True) + EPS) * g


def _in_proj_kernel(x_ref, g_ref, w_ref, cos_ref, sa_ref, sb_ref, o_ref, *, rope_lo, rope_hi, half):
    h = _rms(x_ref[...], g_ref[...]).astype(BF16)
    z = jnp.dot(h, w_ref[...], preferred_element_type=F32)
    if rope_lo > 0:
        o_ref[:, :rope_lo * LANES] = z[:, :rope_lo * LANES].astype(BF16)
    cos = cos_ref[...]
    sa = sa_ref[...]
    sb = sb_ref[...]
    for c in range(rope_lo, rope_hi):
        zc = z[:, c * LANES:(c + 1) * LANES]
        zr = (zc * cos + pltpu.roll(zc, half, 1) * sa
              + pltpu.roll(zc, LANES - half, 1) * sb)
        o_ref[:, c * LANES:(c + 1) * LANES] = zr.astype(BF16)
    if rope_hi * LANES < o_ref.shape[1]:
        o_ref[:, rope_hi * LANES:] = z[:, rope_hi * LANES:].astype(BF16)


def _in_proj(x, g, w, tables, seq, *, bm, rope_cols, half):
    m, d = x.shape
    n = w.shape[1]
    bm = min(bm, seq)
    nsb = seq // bm
    cos, sa, sb = tables
    assert rope_cols[0] % LANES == 0 and rope_cols[1] % LANES == 0 and seq % bm == 0
    kern = functools.partial(_in_proj_kernel, rope_lo=rope_cols[0] // LANES,
                             rope_hi=rope_cols[1] // LANES, half=half)
    tab_spec = pl.BlockSpec((bm, LANES), lambda i: (i % nsb, 0))
    return pl.pallas_call(
        kern,
        grid=(m // bm,),
        in_specs=[
            pl.BlockSpec((bm, d), lambda i: (i, 0)),
            pl.BlockSpec((1, d), lambda i: (0, 0)),
            _resident_spec((d, n), lambda i: (0, 0)),
            tab_spec, tab_spec, tab_spec,
        ],
        out_specs=pl.BlockSpec((bm, n), lambda i: (i, 0)),
        out_shape=jax.ShapeDtypeStruct((m, n), BF16),
        compiler_params=_params(("parallel",)),
        name="in_proj",
    )(x, g, w, cos, sa, sb)


def _rope_tables(seq, head_dim, rot):
    half = rot // 2
    inv = ROPE_THETA ** (-jnp.arange(half, dtype=F32) / half)
    ang = jnp.arange(seq, dtype=F32)[:, None] * inv[None, :]
    c, s = jnp.cos(ang), jnp.sin(ang)
    z_half = jnp.zeros((seq, half), F32)
    z_rest = jnp.zeros((seq, head_dim - rot), F32)
    cos = jnp.concatenate([c, c, jnp.ones((seq, head_dim - rot), F32)], axis=1)
    sa = jnp.concatenate([z_half, s, z_rest], axis=1)
    sb = jnp.concatenate([-s, z_half, z_rest], axis=1)
    reps = LANES // head_dim
    return tuple(jnp.tile(t, (1, reps)) for t in (cos, sa, sb))


def _gmlp_kernel(u_ref, v_ref, vn_ref, ws_ref, bs_ref, o_ref, *, tm):
    for g in range(A_GROUPS):
        cols = slice(g * HEAD_DIM, (g + 1) * HEAD_DIM)
        u = jax.nn.gelu(u_ref[:, cols].astype(F32))
        v = jax.nn.gelu(v_ref[:, cols].astype(F32))
        v = _rms(v, vn_ref[g:g + 1, :]).astype(BF16)
        w = ws_ref[g]
        b = bs_ref[g]
        for c in range(tm // CHUNK):
            rows = slice(c * CHUNK, (c + 1) * CHUNK)
            mixed = jnp.dot(w, v[rows, :], preferred_element_type=F32) + b
            o_ref[rows, cols] = (u[rows, :] * mixed).astype(BF16)


BF16_SUBLANES = 16
VT_ROWS = HEAD_DIM + BF16_SUBLANES

def _vt_kernel(v_ref, o_ref, *, tk):
    row = lax.broadcasted_iota(jnp.int32, (BF16_SUBLANES, tk), 0)
    ones_row = jnp.where(row == 0, 1.0, 0.0).astype(BF16)
    for h in range(B_HEADS):
        v = v_ref[:, h * HEAD_DIM:(h + 1) * HEAD_DIM].astype(F32)
        o_ref[h, :HEAD_DIM, :] = v.T.astype(BF16)
        o_ref[h, HEAD_DIM:, :] = ones_row


def _gmlp_vt_kernel(u_ref, v_ref, vv_ref, vn_ref, ws_ref, bs_ref, o_ref, vt_ref, *, tk):
    _vt_kernel(vv_ref, vt_ref, tk=tk)
    _gmlp_kernel(u_ref, v_ref, vn_ref, ws_ref, bs_ref, o_ref, tm=tk)


def _gmlp_and_v_transposed(z, b, s, vnorm, ws, bs_b, *, tk):
    m = z.shape[0]
    nck = s // tk
    v_block = (2 * A_WIDTH + 2 * B_QK_WIDTH) // B_WIDTH
    return pl.pallas_call(
        functools.partial(_gmlp_vt_kernel, tk=tk),
        grid=(m // tk,),
        in_specs=[
            pl.BlockSpec((tk, A_WIDTH), lambda i: (i, 0)),
            pl.BlockSpec((tk, A_WIDTH), lambda i: (i, 1)),
            pl.BlockSpec((tk, B_WIDTH), lambda i: (i, v_block)),
            pl.BlockSpec((A_GROUPS, HEAD_DIM), lambda i: (0, 0)),
            pl.BlockSpec((A_GROUPS, CHUNK, CHUNK), lambda i: (0, 0, 0)),
            pl.BlockSpec((A_GROUPS, CHUNK, HEAD_DIM), lambda i: (0, 0, 0)),
        ],
        out_specs=[
            pl.BlockSpec((tk, A_WIDTH), lambda i: (i, 0)),
            pl.BlockSpec((None, B_HEADS, None, VT_ROWS, tk), lambda i: (i // nck, 0, i % nck, 0, 0)),
        ],
        out_shape=[
            jax.ShapeDtypeStruct((m, A_WIDTH), BF16),
            jax.ShapeDtypeStruct((b, B_HEADS, nck, VT_ROWS, tk), BF16),
        ],
        compiler_params=_params(("parallel",)),
        name="gmlp_vt",
    )(z, z, z, vnorm, ws, bs_b)


def _diff_attn_kernel(q_ref, k_ref, vt_ref, lam_ref, g_ref, o_ref,
                      qt_ref, s_ref, p_ref, a_ref, acc_ref, *, tq, tk, nck, lam_init):
    q = q_ref[...].astype(F32) * (B_QK_DIM ** -0.5 * LOG2E)
    qt = q.T
    row = lax.broadcasted_iota(jnp.int32, qt.shape, 0)
    qt_ref[:, :tq] = jnp.where(row < B_QK_DIM, qt, 0.0).astype(BF16)
    qt_ref[:, tq:] = jnp.where(row >= B_QK_DIM, qt, 0.0).astype(BF16)

    def scores(c):
        start = c * tk if isinstance(c, int) else pl.multiple_of(c * tk, tk)
        return jnp.dot(k_ref[pl.ds(start, tk), :], qt_ref[...], preferred_element_type=F32)

    def softmax(s, m_old):
        m_new = jnp.maximum(m_old, jnp.max(s, axis=0, keepdims=True))
        alpha = jnp.exp2(m_old - m_new)
        p = jnp.exp2(s - m_new).astype(BF16)
        return p, alpha, m_new

    def accumulate(c, p, alpha):
        acc_ref[...] = alpha * acc_ref[...] + jnp.dot(vt_ref[c], p, preferred_element_type=F32)

    def stage(c, slot, m, with_scores, with_values):
        if with_scores:
            s_ref[1 - slot] = scores(c + 1)
        p, alpha, m = softmax(s_ref[slot], m)
        if with_values:
            accumulate(c - 1, p_ref[1 - slot], a_ref[1 - slot])
        p_ref[slot] = p
        a_ref[slot] = alpha
        return m

    def two_stages(j, m):
        m = stage(2 * j + 1, 1, m, True, True)
        return stage(2 * j + 2, 0, m, True, True)

    s_ref[0] = scores(0)
    acc_ref[...] = jnp.zeros(acc_ref.shape, F32)
    m = jnp.full((1, 2 * tq), NEG_BIG, F32)
    m = stage(0, 0, m, True, False)
    m = lax.fori_loop(0, (nck - 2) // 2, two_stages, m)
    m = stage(nck - 1, 1, m, False, True)
    accumulate(nck - 1, p_ref[1], a_ref[1])

    lv = lam_ref[...]
    lam = (jnp.exp(jnp.sum(lv[0:1] * lv[1:2], axis=1, keepdims=True))
           - jnp.exp(jnp.sum(lv[2:3] * lv[3:4], axis=1, keepdims=True)) + lam_init)
    inv_l = 1.0 / acc_ref[HEAD_DIM:HEAD_DIM + 1, :]
    acc = acc_ref[:HEAD_DIM, :] * inv_l
    ot = acc[:, :tq] - lam * acc[:, tq:]
    o = ot.T
    o_ref[...] = (_rms(o, g_ref[...]) * (1.0 - lam_init)).astype(BF16)


def _diff_attn(z3, vt, lam_vecs, subnorm, lam_init, *, tq, tk):
    b, s, _ = z3.shape
    nck = s // tk
    assert s % tq == 0 and s % tk == 0 and nck % 2 == 0
    q_block = 2 * A_WIDTH // LANES
    k_block = (2 * A_WIDTH + B_QK_WIDTH) // LANES
    kern = functools.partial(_diff_attn_kernel, tq=tq, tk=tk, nck=nck, lam_init=lam_init)
    return pl.pallas_call(
        kern,
        grid=(b, B_HEADS, s // tq),
        in_specs=[
            pl.BlockSpec((None, tq, LANES), lambda i, h, t: (i, t, q_block + h)),
            pl.BlockSpec((None, s, LANES), lambda i, h, t: (i, 0, k_block + h)),
            pl.BlockSpec((None, None, nck, VT_ROWS, tk), lambda i, h, t: (i, h, 0, 0, 0)),
            pl.BlockSpec((4, B_QK_DIM), lambda i, h, t: (0, 0)),
            pl.BlockSpec((1, HEAD_DIM), lambda i, h, t: (0, 0)),
        ],
        out_specs=pl.BlockSpec((None, tq, HEAD_DIM), lambda i, h, t: (i, t, h)),
        out_shape=jax.ShapeDtypeStruct((b, s, B_WIDTH), BF16),
        scratch_shapes=[
            pltpu.VMEM((HEAD_DIM, 2 * tq), BF16),
            pltpu.VMEM((2, tk, 2 * tq), F32),
            pltpu.VMEM((2, tk, 2 * tq), BF16),
            pltpu.VMEM((2, 1, 2 * tq), F32),
            pltpu.VMEM((VT_ROWS, 2 * tq), F32),
        ],
        compiler_params=_params(("parallel", "parallel", "arbitrary")),
        name="diff_attn",
    )(z3, z3, vt, lam_vecs, subnorm)


def _win_attn_kernel(sink_ref, q_ref, kp_ref, kc_ref, kn_ref, vp_ref, vc_ref, vn_ref, o_ref, *, nb):
    n = pl.program_id(1)
    rows = C_GROUP * CHUNK
    r = lax.broadcasted_iota(jnp.int32, (rows, 3 * CHUNK), 0) % CHUNK
    c = lax.broadcasted_iota(jnp.int32, (rows, 3 * CHUNK), 1)
    rel = c - CHUNK - r
    valid = jnp.logical_and(rel <= C_WINDOW, rel >= -C_WINDOW)
    valid = jnp.logical_and(valid, jnp.logical_or(n > 0, c >= CHUNK))
    valid = jnp.logical_and(valid, jnp.logical_or(n < nb - 1, c < 2 * CHUNK))
    ones_block = jnp.ones((3 * CHUNK, LANES), BF16)
    scale = HEAD_DIM ** -0.5 * LOG2E
    for kh in range(C_KV_HEADS):
        kcols = slice(kh * HEAD_DIM, (kh + 1) * HEAD_DIM)
        kband = jnp.concatenate([kp_ref[:, kcols], kc_ref[:, kcols], kn_ref[:, kcols]], axis=0)
        vband = jnp.concatenate([vp_ref[:, kcols], vc_ref[:, kcols], vn_ref[:, kcols]], axis=0)
        v_aug = jnp.concatenate([vband, ones_block], axis=1)
        heads = [kh * C_GROUP + g for g in range(C_GROUP)]
        q = jnp.concatenate([q_ref[:, h * HEAD_DIM:(h + 1) * HEAD_DIM] for h in heads], axis=0)
        q = (q.astype(F32) * scale).astype(BF16)
        sink = jnp.concatenate([jnp.full((CHUNK, LANES), sink_ref[h] * LOG2E, F32) for h in heads], axis=0)
        s = lax.dot_general(q, kband, (((1,), (1,)), ((), ())), preferred_element_type=F32)
        s = jnp.where(valid, s, NEG_BIG)
        m = jnp.maximum(jnp.broadcast_to(jnp.max(s, axis=-1, keepdims=True), sink.shape), sink)
        p = jnp.exp2(s - jnp.concatenate([m, m, m], axis=1)).astype(BF16)
        o_aug = jnp.dot(p, v_aug, preferred_element_type=F32)
        denom = o_aug[:, HEAD_DIM:] + jnp.exp2(sink - m)
        o = o_aug[:, :HEAD_DIM] / denom
        for g, h in enumerate(heads):
            o_ref[:, h * HEAD_DIM:(h + 1) * HEAD_DIM] = o[g * CHUNK:(g + 1) * CHUNK, :].astype(BF16)


def _win_attn(z3, sink):
    b, s, _ = z3.shape
    nb = s // CHUNK
    kv_w = C_KV_HEADS * HEAD_DIM
    k_block = C_HEADS * HEAD_DIM // kv_w
    v_block = k_block + 1
    prev = lambda n: jnp.maximum(n - 1, 0)
    nxt = lambda n: jnp.minimum(n + 1, nb - 1)

    def kv_spec(blk, f):
        return pl.BlockSpec((None, CHUNK, kv_w), lambda i, n, sk: (i, f(n), blk))

    same = lambda n: n
    grid_spec = pltpu.PrefetchScalarGridSpec(
        num_scalar_prefetch=1,
        grid=(b, nb),
        in_specs=[
            pl.BlockSpec((None, CHUNK, C_HEADS * HEAD_DIM), lambda i, n, sk: (i, n, 0)),
            kv_spec(k_block, prev), kv_spec(k_block, same), kv_spec(k_block, nxt),
            kv_spec(v_block, prev), kv_spec(v_block, same), kv_spec(v_block, nxt),
        ],
        out_specs=pl.BlockSpec((None, CHUNK, C_HEADS * HEAD_DIM), lambda i, n, sk: (i, n, 0)),
    )
    return pl.pallas_call(
        functools.partial(_win_attn_kernel, nb=nb),
        grid_spec=grid_spec,
        out_shape=jax.ShapeDtypeStruct((b, s, C_HEADS * HEAD_DIM), BF16),
        compiler_params=_params(("parallel", "parallel")),
        name="win_attn",
    )(sink, z3, z3, z3, z3, z3, z3, z3)


def _out_proj_kernel(*refs, n_in):
    a_refs = refs[:n_in]
    w_refs = refs[n_in:2 * n_in]
    x_ref, g_ref, o_ref = refs[2 * n_in:]
    acc = jnp.dot(a_refs[0][...], w_refs[0][...], preferred_element_type=F32)
    for a_ref, w_ref in zip(a_refs[1:], w_refs[1:]):
        acc = acc + jnp.dot(a_ref[...], w_ref[...], preferred_element_type=F32)
    o_ref[...] = x_ref[...] + _rms(acc, g_ref[...])


def _out_proj(acts, ws, x, g, *, bm=OUT_PROJ_ROWS):
    m, d = x.shape
    bm = min(bm, m)
    n_in = len(acts)
    in_specs = [pl.BlockSpec((bm, a.shape[1]), lambda i: (i, 0)) for a in acts]
    in_specs += [_resident_spec(w.shape, lambda i: (0, 0)) for w in ws]
    in_specs += [pl.BlockSpec((bm, d), lambda i: (i, 0)), pl.BlockSpec((1, d), lambda i: (0, 0))]
    return pl.pallas_call(
        functools.partial(_out_proj_kernel, n_in=n_in),
        grid=(m // bm,),
        in_specs=in_specs,
        out_specs=pl.BlockSpec((bm, d), lambda i: (i, 0)),
        out_shape=jax.ShapeDtypeStruct((m, d), F32),
        compiler_params=_params(("parallel",)),
        name="out_proj",
    )(*acts, *ws, x, g)


def _mlp_kernel(x_ref, g1_ref, wu_ref, wd_ref, g2_ref, o_ref, h_ref, acc_ref):
    f = pl.program_id(1)
    last_f = pl.num_programs(1) - 1

    def step(first, last):
        if first:
            h = _rms(x_ref[...], g1_ref[...]).astype(BF16)
            h_ref[...] = h
        else:
            h = h_ref[...]
        a = jnp.dot(h, wu_ref[...], preferred_element_type=F32)
        a = jnp.square(jnp.maximum(a, 0.0)).astype(BF16)
        part = jnp.dot(a, wd_ref[...], preferred_element_type=F32)
        total = part if first else acc_ref[...] + part
        if last:
            o_ref[...] = x_ref[...] + _rms(total, g2_ref[...])
        else:
            acc_ref[...] = total

    pl.when(f == 0)(functools.partial(step, True, False))
    pl.when(jnp.logical_and(f > 0, f < last_f))(functools.partial(step, False, False))
    pl.when(f == last_f)(functools.partial(step, False, True))


def _mlp(x, g1, wu, wd, g2, *, bm=MLP_ROWS, tf=MLP_HIDDEN_CHUNK):
    m, d = x.shape
    dff = wu.shape[1]
    bm = min(bm, m)
    assert m % bm == 0 and dff % tf == 0 and dff // tf >= 2
    return pl.pallas_call(
        _mlp_kernel,
        grid=(m // bm, dff // tf),
        in_specs=[
            pl.BlockSpec((bm, d), lambda i, f: (i, 0)),
            pl.BlockSpec((1, d), lambda i, f: (0, 0)),
            pl.BlockSpec((d, tf), lambda i, f: (0, f)),
            pl.BlockSpec((tf, d), lambda i, f: (f, 0)),
            pl.BlockSpec((1, d), lambda i, f: (0, 0)),
        ],
        out_specs=pl.BlockSpec((bm, d), lambda i, f: (i, 0)),
        out_shape=jax.ShapeDtypeStruct((m, d), F32),
        scratch_shapes=[pltpu.VMEM((bm, d), BF16), pltpu.VMEM((bm, d), F32)],
        compiler_params=_params(("parallel", "arbitrary")),
        name="mlp",
    )(x, g1, wu, wd, g2)


def _pair_heads(w):
    d = w.shape[0]
    return w.reshape(d, 2, B_HEADS, B_QK_DIM).transpose(0, 2, 1, 3).reshape(d, B_QK_WIDTH)


def _prep_weights(p):
    row = lambda a: a.reshape(a.shape[0], 1, a.shape[-1]).astype(F32)
    w_in_ab = p["w_in_ab"].astype(BF16)
    q0, k0, v0 = 2 * A_WIDTH, 2 * A_WIDTH + B_QK_WIDTH, 2 * A_WIDTH + 2 * B_QK_WIDTH
    w_in_ab = jnp.concatenate([
        w_in_ab[:, :, :q0],
        jax.vmap(_pair_heads)(w_in_ab[:, :, q0:k0]),
        jax.vmap(_pair_heads)(w_in_ab[:, :, k0:v0]),
        w_in_ab[:, :, v0:],
    ], axis=-1)
    return dict(
        norm_mix_pre=row(p["norm_mix_pre"]), norm_mix_post=row(p["norm_mix_post"]),
        norm_mlp_pre=row(p["norm_mlp_pre"]), norm_mlp_post=row(p["norm_mlp_post"]),
        w_in_ab=w_in_ab,
        w_out_ab=p["w_out_ab"].astype(BF16),
        a_vnorm=p["a_vnorm"].astype(F32),
        a_ws=p["a_ws"].astype(BF16),
        a_bs_b=jnp.broadcast_to(p["a_bs"].astype(F32)[..., None],
                                p["a_bs"].shape + (HEAD_DIM,)),
        b_lambda=p["b_lambda"].astype(F32),
        b_subnorm=p["b_subnorm"].astype(F32)[:, None, :],
        w_in_c=p["w_in_c"].astype(BF16),
        w_out_c=p["w_out_c"].astype(BF16),
        c_sink=p["c_sink"].astype(F32),
        w_up=p["w_up"].astype(BF16),
        w_down=p["w_down"].astype(BF16),
    )


def _trunk(x3, p):
    b, s, d = x3.shape
    m = b * s
    x = x3.reshape(m, d)
    tq = min(ATTN_QUERY_TILE, s)
    tk = min(ATTN_KEY_CHUNK, s // 2)
    tab_ab = _rope_tables(s, B_QK_DIM, B_QK_DIM // ROPE_FRACTION)
    tab_c = _rope_tables(s, HEAD_DIM, HEAD_DIM // ROPE_FRACTION)
    for i in range(DEPTH):
        j = i // 2
        if i % 2 == 0:
            lam_init = 0.8 - 0.6 * math.exp(-0.3 * i)
            z = _in_proj(x, p["norm_mix_pre"][i], p["w_in_ab"][j], tab_ab, s, bm=IN_PROJ_AB_ROWS,
                         rope_cols=(2 * A_WIDTH, 2 * A_WIDTH + 2 * B_QK_WIDTH),
                         half=B_QK_DIM // ROPE_FRACTION // 2)
            a_out, vt = _gmlp_and_v_transposed(z, b, s, p["a_vnorm"][j], p["a_ws"][j],
                                               p["a_bs_b"][j], tk=tk)
            z3 = z.reshape(b, s, AB_IN)
            b_out = _diff_attn(z3, vt, p["b_lambda"][j], p["b_subnorm"][j], lam_init, tq=tq, tk=tk)
            w_out = p["w_out_ab"][j]
            x = _out_proj([a_out, b_out.reshape(m, B_WIDTH)], [w_out[:A_WIDTH], w_out[A_WIDTH:]],
                          x, p["norm_mix_post"][i])
        else:
            z = _in_proj(x, p["norm_mix_pre"][i], p["w_in_c"][j], tab_c, s, bm=IN_PROJ_C_ROWS,
                         rope_cols=(0, (C_HEADS + C_KV_HEADS) * HEAD_DIM),
                         half=HEAD_DIM // ROPE_FRACTION // 2)
            c_out = _win_attn(z.reshape(b, s, C_IN), p["c_sink"][j])
            x = _out_proj([c_out.reshape(m, D_MODEL)], [p["w_out_c"][j]], x, p["norm_mix_post"][i])
        x = _mlp(x, p["norm_mlp_pre"][i], p["w_up"][i], p["w_down"][i], p["norm_mlp_post"][i])
    return x.reshape(b, s, d)


def kernel(x_prompt, x_sample, norm_mix_pre, norm_mix_post, norm_mlp_pre, norm_mlp_post,
           w_in_ab, w_out_ab, a_vnorm, a_ws, a_bs, b_lambda, b_subnorm,
           w_in_c, w_out_c, c_sink, w_up, w_down):
    p = _prep_weights(dict(
        norm_mix_pre=norm_mix_pre, norm_mix_post=norm_mix_post,
        norm_mlp_pre=norm_mlp_pre, norm_mlp_post=norm_mlp_post,
        w_in_ab=w_in_ab, w_out_ab=w_out_ab, a_vnorm=a_vnorm, a_ws=a_ws, a_bs=a_bs,
        b_lambda=b_lambda, b_subnorm=b_subnorm, w_in_c=w_in_c, w_out_c=w_out_c,
        c_sink=c_sink, w_up=w_up, w_down=w_down))
    return (_trunk(x_prompt, p), _trunk(x_sample, p))
```
